```python
import math
import jax, jax.numpy as jnp
from jax import lax
import numpy as np

D_MODEL = 1024
BATCH = 16
SEQ = 2048
DEPTH = 1
DEC_BATCH = 128
DEC_SEQ = 4
PAST_LEN = 8192
PAGE_SIZE = 128

SB_HEADS = 8
SB_HD = 64
DIFF_HEADS = 4
DIFF_HD = 64
ROPE_DIM = DIFF_HD // 4
ROPE_THETA = 500000.0
MEM_TOKENS = 256
MEM_HEADS = 4
MEM_HD = D_MODEL // MEM_HEADS
D_FF = 2816
CONV_W = 3
Q_BLOCK = 128
EPS = 1e-6
SB_W = SB_HEADS * SB_HD
DIFF_QK_W = DIFF_HEADS * 2 * DIFF_HD
DIFF_V_W = DIFF_HEADS * 2 * DIFF_HD
MIX_W = SB_W + DIFF_V_W
IN_W = 3 * SB_W + 2 * DIFF_QK_W + DIFF_V_W
NEG = -1e30

kernel_name = "stickbreak_diffattn_hymba_step"


def rmsnorm(x, g):
    xf = x.astype(jnp.float32)
    y = xf * lax.rsqrt(jnp.mean(xf * xf, axis=-1, keepdims=True) + EPS)
    return (y * g.astype(jnp.float32)).astype(x.dtype)


def rope(x, pos):
    half = ROPE_DIM // 2
    inv_freq = ROPE_THETA ** (-jnp.arange(half, dtype=jnp.float32) * 2.0 / ROPE_DIM)
    ang = pos.astype(jnp.float32)[:, None] * inv_freq[None, :]
    shape = (1, pos.shape[0]) + (1,) * (x.ndim - 3) + (half,)
    cos = jnp.cos(ang).reshape(shape)
    sin = jnp.sin(ang).reshape(shape)
    xf = x.astype(jnp.float32)
    x1 = xf[..., :half]
    x2 = xf[..., half:ROPE_DIM]
    out = jnp.concatenate([x1 * cos - x2 * sin, x2 * cos + x1 * sin, xf[..., ROPE_DIM:]], axis=-1)
    return out.astype(x.dtype)


def mixer_inputs(xn, w_in, pos):
    B, T, _ = xn.shape
    u = xn @ w_in
    i0, i1, i2 = SB_W, 2 * SB_W, 3 * SB_W
    i3, i4 = i2 + DIFF_QK_W, i2 + 2 * DIFF_QK_W
    sb_q = u[..., :i0].reshape(B, T, SB_HEADS, SB_HD)
    sb_k = u[..., i0:i1].reshape(B, T, SB_HEADS, SB_HD)
    sb_v = u[..., i1:i2].reshape(B, T, SB_HEADS, SB_HD)
    d_q = rope(u[..., i2:i3].reshape(B, T, DIFF_HEADS, 2, DIFF_HD), pos)
    d_k = rope(u[..., i3:i4].reshape(B, T, DIFF_HEADS, 2, DIFF_HD), pos)
    d_v = u[..., i4:].reshape(B, T, DIFF_HEADS, 2 * DIFF_HD)
    return sb_q, sb_k, sb_v, d_q, d_k, d_v


def stick_breaking(q, k, v, q_pos, k_pos):
    z = jnp.einsum("bqhd,bkhd->bhqk", q.astype(jnp.float32), k.astype(jnp.float32)) / math.sqrt(SB_HD)
    mask = k_pos[None, :] < q_pos[:, None]
    log_1m = jnp.where(mask, -jax.nn.softplus(z), 0.0)
    after = lax.cumsum(log_1m, axis=3, reverse=True) - log_1m
    a = jnp.where(mask, jnp.exp(jax.nn.log_sigmoid(z) + after), 0.0)
    return jnp.einsum("bhqk,bkhd->bqhd", a, v.astype(jnp.float32))


def diff_attention(q, k, v, q_pos, k_pos, lam):
    s = jnp.einsum("bqhcd,bkhcd->bhcqk", q.astype(jnp.float32), k.astype(jnp.float32)) / math.sqrt(DIFF_HD)
    mask = k_pos[None, :] <= q_pos[:, None]
    p = jax.nn.softmax(jnp.where(mask, s, NEG), axis=-1)
    w = p[:, :, 0] - lam * p[:, :, 1]
    return jnp.einsum("bhqk,bkhe->bqhe", w, v.astype(jnp.float32))


def diff_lambda(lq1, lk1, lq2, lk2, lambda_init):
    f = jnp.float32
    return (jnp.exp(jnp.sum(lq1.astype(f) * lk1.astype(f))) - jnp.exp(jnp.sum(lq2.astype(f) * lk2.astype(f))) + lambda_init)


def mixer_output(o_sb, o_diff, subln_g, lambda_init, w_out, dtype):
    B, T = o_sb.shape[:2]
    o_diff = rmsnorm(o_diff, subln_g) * (1.0 - lambda_init)
    o = jnp.concatenate([o_sb.reshape(B, T, SB_W), o_diff.reshape(B, T, DIFF_V_W)], axis=-1).astype(dtype)
    return o @ w_out


def mem_attention(hn, mk, mv, wq, wo):
    B, T, _ = hn.shape
    q = (hn @ wq).reshape(B, T, MEM_HEADS, MEM_HD)
    s = jnp.einsum("bqhd,bmhd->bhqm", q.astype(jnp.float32), mk.astype(jnp.float32)) / math.sqrt(MEM_HD)
    p = jax.nn.softmax(s, axis=-1)
    o = jnp.einsum("bhqm,bmhd->bqhd", p, mv.astype(jnp.float32)).reshape(B, T, MEM_HEADS * MEM_HD)
    return o.astype(hn.dtype) @ wo


def mem_kv(mem, g, wk, wv):
    B = mem.shape[0]
    mn = rmsnorm(mem, g)
    mk = (mn @ wk).reshape(B, MEM_TOKENS, MEM_HEADS, MEM_HD)
    mv = (mn @ wv).reshape(B, MEM_TOKENS, MEM_HEADS, MEM_HD)
    return mk, mv


def conv_ffn(hn, conv_prev, w_up, w_conv, b_conv, w_down):
    T = hn.shape[1]
    u = hn @ w_up
    ext = jnp.concatenate([conv_prev.astype(u.dtype), u], axis=1)
    c = b_conv
    for i in range(CONV_W):
        c = c + ext[:, i:i + T] * w_conv[i]
    gate, val = jnp.split(c, 2, axis=-1)
    out = (jax.nn.silu(gate) * val) @ w_down
    return out, ext[:, ext.shape[1] - (CONV_W - 1):]


def mem_and_ffn(h, mk, mv, conv_prev, g_mem, wq, wo, g_ffn, w_up, w_conv, b_conv, w_down):
    h = h + mem_attention(rmsnorm(h, g_mem), mk, mv, wq, wo)
    f, conv_new = conv_ffn(rmsnorm(h, g_ffn), conv_prev, w_up, w_conv, b_conv, w_down)
    return h + f, conv_new


def setup_inputs(seed: int = 0) -> dict:
    key = jax.random.key(seed)
    ks = iter(jax.random.split(key, 40))
    f = jnp.float32
    n_pages = PAST_LEN // PAGE_SIZE
    n_used = DEC_BATCH * n_pages
    n_pool = n_used + n_used // 4
    nrm = lambda shape, s=1.0: jax.random.normal(next(ks), shape, f) * s
    gain = lambda shape: 1.0 + 0.05 * jax.random.normal(next(ks), shape, f)
    perm = jax.random.permutation(next(ks), n_pool)[:n_used]
    page_table = perm.reshape(DEC_BATCH, n_pages).astype(jnp.int32)
    return {
        "x_prompt": nrm((BATCH, SEQ, D_MODEL)),
        "x_sample": nrm((DEC_BATCH, DEC_SEQ, D_MODEL)),
        "cache_sb_k": nrm((DEPTH, n_pool, PAGE_SIZE, SB_HEADS, SB_HD)),
        "cache_sb_v": nrm((DEPTH, n_pool, PAGE_SIZE, SB_HEADS, SB_HD)),
        "cache_diff_k": nrm((DEPTH, n_pool, PAGE_SIZE, DIFF_HEADS, 2, DIFF_HD)),
        "cache_diff_v": nrm((DEPTH, n_pool, PAGE_SIZE, DIFF_HEADS, 2 * DIFF_HD)),
        "cache_mem_k": nrm((DEPTH, DEC_BATCH, MEM_TOKENS, MEM_HEADS, MEM_HD)),
        "cache_mem_v": nrm((DEPTH, DEC_BATCH, MEM_TOKENS, MEM_HEADS, MEM_HD)),
        "state_conv": nrm((DEPTH, DEC_BATCH, CONV_W - 1, 2 * D_FF)),
        "page_table": page_table,
        "mem_prompt": nrm((BATCH, MEM_TOKENS, D_MODEL)),
        "norm_mix": gain((DEPTH, D_MODEL)),
        "w_in": nrm((DEPTH, D_MODEL, IN_W), D_MODEL ** -0.5),
        "lambda_q1": nrm((DEPTH, DIFF_HD), 0.1),
        "lambda_k1": nrm((DEPTH, DIFF_HD), 0.1),
        "lambda_q2": nrm((DEPTH, DIFF_HD), 0.1),
        "lambda_k2": nrm((DEPTH, DIFF_HD), 0.1),
        "diff_subln": gain((DEPTH, 2 * DIFF_HD)),
        "w_out": nrm((DEPTH, MIX_W, D_MODEL), MIX_W ** -0.5),
        "norm_mem_q": gain((DEPTH, D_MODEL)),
        "norm_mem_kv": gain((DEPTH, D_MODEL)),
        "w_mem_q": nrm((DEPTH, D_MODEL, MEM_HEADS * MEM_HD), D_MODEL ** -0.5),
        "w_mem_k": nrm((DEPTH, D_MODEL, MEM_HEADS * MEM_HD), D_MODEL ** -0.5),
        "w_mem_v": nrm((DEPTH, D_MODEL, MEM_HEADS * MEM_HD), D_MODEL ** -0.5),
        "w_mem_o": nrm((DEPTH, MEM_HEADS * MEM_HD, D_MODEL), (MEM_HEADS * MEM_HD) ** -0.5),
        "norm_ffn": gain((DEPTH, D_MODEL)),
        "w_up": nrm((DEPTH, D_MODEL, 2 * D_FF), D_MODEL ** -0.5),
        "w_conv": nrm((DEPTH, CONV_W, 2 * D_FF), CONV_W ** -0.5),
        "b_conv": nrm((DEPTH, 2 * D_FF), 0.01),
        "w_down": nrm((DEPTH, D_FF, D_MODEL), D_FF ** -0.5),
        "norm_final": gain((D_MODEL,)),
    }


def reference(x_prompt, x_sample, cache_sb_k, cache_sb_v, cache_diff_k, cache_diff_v, cache_mem_k, cache_mem_v, state_conv, page_table, mem_prompt, norm_mix, w_in, lambda_q1, lambda_k1, lambda_q2, lambda_k2, diff_subln, w_out, norm_mem_q, norm_mem_kv, w_mem_q, w_mem_k, w_mem_v, w_mem_o, norm_ffn, w_up, w_conv, b_conv, w_down, norm_final):
    Bp, Tp, _ = x_prompt.shape
    Bs, Ts, _ = x_sample.shape
    n_pages = PAST_LEN // PAGE_SIZE
    nb = Tp // Q_BLOCK
    pos_p = jnp.arange(Tp, dtype=jnp.int32)
    pos_s = PAST_LEN + jnp.arange(Ts, dtype=jnp.int32)
    pos_all = jnp.arange(PAST_LEN + Ts, dtype=jnp.int32)

    def gather(pool):
        g = pool[page_table]
        return g.reshape((Bs, n_pages * PAGE_SIZE) + pool.shape[2:])

    def to_blocks(a):
        return a.reshape((Bp, nb, Q_BLOCK) + a.shape[2:]).swapaxes(0, 1)

    def from_blocks(a):
        return a.swapaxes(0, 1).reshape((Bp, Tp) + a.shape[3:])

    hp, hs = x_prompt, x_sample
    p_sbk, p_sbv, p_dk, p_dv, p_mk, p_mv, p_conv = [], [], [], [], [], [], []
    s_sbk, s_sbv, s_dk, s_dv, s_conv = [], [], [], [], []
    for l in range(DEPTH):
        lambda_init = 0.8 - 0.6 * math.exp(-0.3 * l)
        lam = diff_lambda(lambda_q1[l], lambda_k1[l], lambda_q2[l], lambda_k2[l], lambda_init)

        sb_q, sb_k, sb_v, d_q, d_k, d_v = mixer_inputs(rmsnorm(hp, norm_mix[l]), w_in[l], pos_p)

        def block(args, sb_k=sb_k, sb_v=sb_v, d_k=d_k, d_v=d_v, lam=lam):
            qs, qd, qp = args
            return (stick_breaking(qs, sb_k, sb_v, qp, pos_p), diff_attention(qd, d_k, d_v, qp, pos_p, lam))

        o_sb, o_d = lax.map(block, (to_blocks(sb_q), to_blocks(d_q), pos_p.reshape(nb, Q_BLOCK)))
        hp = hp + mixer_output(from_blocks(o_sb), from_blocks(o_d), diff_subln[l], lambda_init, w_out[l], hp.dtype)
        mk_p, mv_p = mem_kv(mem_prompt, norm_mem_kv[l], w_mem_k[l], w_mem_v[l])
        conv0 = jnp.zeros((Bp, CONV_W - 1, 2 * D_FF), hp.dtype)
        hp, conv_p = mem_and_ffn(hp, mk_p, mv_p, conv0, norm_mem_q[l], w_mem_q[l], w_mem_o[l], norm_ffn[l], w_up[l], w_conv[l], b_conv[l], w_down[l])
        p_sbk.append(sb_k); p_sbv.append(sb_v); p_dk.append(d_k); p_dv.append(d_v)
        p_mk.append(mk_p); p_mv.append(mv_p); p_conv.append(conv_p)

        sq, sk, sv, dq, dk, dv = mixer_inputs(rmsnorm(hs, norm_mix[l]), w_in[l], pos_s)
        k_sb = jnp.concatenate([gather(cache_sb_k[l]), sk], axis=1)
        v_sb = jnp.concatenate([gather(cache_sb_v[l]), sv], axis=1)
        k_d = jnp.concatenate([gather(cache_diff_k[l]), dk], axis=1)
        v_d = jnp.concatenate([gather(cache_diff_v[l]), dv], axis=1)
        o_sb_s = stick_breaking(sq, k_sb, v_sb, pos_s, pos_all)
        o_d_s = diff_attention(dq, k_d, v_d, pos_s, pos_all, lam)
        hs = hs + mixer_output(o_sb_s, o_d_s, diff_subln[l], lambda_init, w_out[l], hs.dtype)
        hs, conv_s = mem_and_ffn(hs, cache_mem_k[l], cache_mem_v[l], state_conv[l], norm_mem_q[l], w_mem_q[l], w_mem_o[l], norm_ffn[l], w_up[l], w_conv[l], b_conv[l], w_down[l])
        s_sbk.append(sk); s_sbv.append(sv); s_dk.append(dk); s_dv.append(dv); s_conv.append(conv_s)

    y_prompt = rmsnorm(hp, norm_final)
    y_sample = rmsnorm(hs, norm_final)
    st = lambda xs: jnp.stack(xs, axis=0)
    return (y_prompt, y_sample, st(p_sbk), st(p_sbv), st(p_dk), st(p_dv), st(p_mk), st(p_mv), st(p_conv), st(s_sbk), st(s_sbv), st(s_dk), st(s_dv), st(s_conv))
```

```python
import functools
import math

import jax
import jax.numpy as jnp
from jax import lax
from jax.experimental import pallas as pl
from jax.experimental.pallas import tpu as pltpu

F32 = jnp.float32
BF16 = jnp.bfloat16

EPS = 1e-6
ROPE_DIM = 16
ROPE_THETA = 500000.0
CONV_W = 3
NEG = -1e30

LANES = 128
SUBLANES = 8
MXU_DIM = 256
VMEM_LIMIT = 56 * 1024 * 1024

TOKEN_TILE = 512
ATTN_TILE = 256
PAGES_PER_STEP = 8
FF_CHUNK = 256


def _dot(a, b):
    return jnp.dot(a, b, preferred_element_type=F32)


def _dot_nt(a, b):
    return lax.dot_general(a, b, (((1,), (1,)), ((), ())), preferred_element_type=F32)


def _rms(x, g):
    return x * lax.rsqrt(jnp.mean(x * x, axis=-1, keepdims=True) + EPS) * g


def _resident(shape):
    nd = len(shape)
    return pl.BlockSpec(shape, lambda *_: (0,) * nd, pipeline_mode=pl.Buffered(1))


def _params(sem):
    return pltpu.CompilerParams(dimension_semantics=sem, vmem_limit_bytes=VMEM_LIMIT)


def _split_hi_lo(x):
    hi = x.astype(BF16)
    lo = (x - hi.astype(F32)).astype(BF16)
    return hi, lo


def _strict_lower_ones(n):
    r = lax.broadcasted_iota(jnp.int32, (n, n), 0)
    c = lax.broadcasted_iota(jnp.int32, (n, n), 1)
    return jnp.where(r > c, 1.0, 0.0).astype(BF16)


def _neg_softplus(s):
    return -(jnp.maximum(s, 0.0) + jnp.log(1.0 + jnp.exp(-jnp.abs(s))))


def _norm_matmul_kernel(*refs, segs, has_rope):
    x_ref, g_ref, w_ref = refs[:3]
    rope_ref = refs[3] if has_rope else None
    outs = refs[4:] if has_rope else refs[3:]
    xn = _rms(x_ref[...], g_ref[...]).astype(BF16)
    oi = 0
    for c0, width, scale, rope, want_f32, want_bf16 in segs:
        u = _dot(xn, w_ref[:, c0:c0 + width])
        if rope:
            cos = rope_ref[:, 0:LANES]
            sin_lo = rope_ref[:, LANES:2 * LANES]
            sin_hi = rope_ref[:, 2 * LANES:3 * LANES]
            half = ROPE_DIM // 2
            pieces = []
            for j in range(width // LANES):
                uj = u[:, j * LANES:(j + 1) * LANES]
                pieces.append(uj * cos + pltpu.roll(uj, LANES - half, 1) * sin_lo
                              + pltpu.roll(uj, half, 1) * sin_hi)
            u = jnp.concatenate(pieces, axis=1)
        if want_f32:
            outs[oi][...] = u
            oi += 1
        if want_bf16:
            outs[oi][...] = (u * scale).astype(BF16)
            oi += 1


def _norm_matmul(x, g, w_bf16, segs, rope_tab=None):
    m, d = x.shape
    tm = min(TOKEN_TILE, m)
    assert m % tm == 0
    in_specs = [pl.BlockSpec((tm, d), lambda i: (i, 0)), _resident((1, d)), _resident(w_bf16.shape)]
    args = [x, g.reshape(1, d), w_bf16]
    if rope_tab is not None:
        nrt = rope_tab.shape[0] // tm
        assert rope_tab.shape[0] % tm == 0
        in_specs.append(pl.BlockSpec((tm, 3 * LANES), lambda i: (i % nrt, 0)))
        args.append(rope_tab)
    out_shape, out_specs = [], []
    for _, width, _, _, want_f32, want_bf16 in segs:
        for want, dt in ((want_f32, F32), (want_bf16, BF16)):
            if want:
                out_shape.append(jax.ShapeDtypeStruct((m, width), dt))
                out_specs.append(pl.BlockSpec((tm, width), lambda i: (i, 0)))
    return pl.pallas_call(
        functools.partial(_norm_matmul_kernel, segs=tuple(segs), has_rope=rope_tab is not None),
        grid=(m // tm,), in_specs=in_specs, out_specs=out_specs, out_shape=out_shape,
        compiler_params=_params(("parallel",)), name="norm_matmul",
    )(*args)


def _rope_table(pos):
    half = ROPE_DIM // 2
    inv_freq = ROPE_THETA ** (-jnp.arange(half, dtype=F32) * 2.0 / ROPE_DIM)
    ang = pos.astype(F32)[:, None] * inv_freq[None, :]
    cos, sin = jnp.cos(ang), jnp.sin(ang)
    n = pos.shape[0]
    ones = jnp.ones((n, 64 - ROPE_DIM), F32)
    zeros = jnp.zeros((n, 64 - ROPE_DIM), F32)
    zh = jnp.zeros((n, half), F32)
    c64 = jnp.concatenate([cos, cos, ones], axis=1)
    lo64 = jnp.concatenate([-sin, zh, zeros], axis=1)
    hi64 = jnp.concatenate([zh, sin, zeros], axis=1)
    rep = LANES // 64
    return jnp.concatenate([jnp.tile(c64, (1, rep)), jnp.tile(lo64, (1, rep)), jnp.tile(hi64, (1, rep))], axis=1)


def _matmul_residual_kernel(a_ref, w_ref, r_ref, o_ref):
    o_ref[...] = r_ref[...] + _dot(a_ref[...].astype(BF16), w_ref[...])


def _matmul_residual(a, w_bf16, res):
    m, k = a.shape
    n = w_bf16.shape[1]
    tm = min(TOKEN_TILE, m)
    assert m % tm == 0
    return pl.pallas_call(
        _matmul_residual_kernel, grid=(m // tm,),
        in_specs=[pl.BlockSpec((tm, k), lambda i: (i, 0)), _resident(w_bf16.shape),
                  pl.BlockSpec((tm, n), lambda i: (i, 0))],
        out_specs=pl.BlockSpec((tm, n), lambda i: (i, 0)),
        out_shape=jax.ShapeDtypeStruct((m, n), F32),
        compiler_params=_params(("parallel",)), name="matmul_residual",
    )(a, w_bf16, res)


def _diff_lambda(lam_ref, lambda_init):
    a = jnp.sum(lam_ref[0:1, :] * lam_ref[1:2, :], axis=-1, keepdims=True)
    b = jnp.sum(lam_ref[2:3, :] * lam_ref[3:4, :], axis=-1, keepdims=True)
    return jnp.exp(a) - jnp.exp(b) + lambda_init


def _sb_weights(s, run, tri, mask):
    n = s.shape[1]
    blk = tri.shape[0]
    nb = n // blk
    log1m = _neg_softplus(s)
    log_sig = s + log1m
    if mask is not None:
        log1m = jnp.where(mask, log1m, 0.0)
    stacked = jnp.concatenate([log1m[:, b * blk:(b + 1) * blk] for b in range(nb)], axis=0)
    hi, lo = _split_hi_lo(stacked)
    loc = _dot(hi, tri) + _dot(lo, tri)
    rows = s.shape[0]
    after = []
    for b in reversed(range(nb)):
        loc_b = loc[b * rows:(b + 1) * rows]
        after.append(loc_b + run)
        run = run + loc_b[:, 0:1] + log1m[:, b * blk:b * blk + 1]
    after = jnp.concatenate(after[::-1], axis=1) if nb > 1 else after[0]
    a = jnp.exp(log_sig + after)
    if mask is not None:
        a = jnp.where(mask, a, 0.0)
    return a, run


def _softmax_step(s, m_prev, l_prev):
    m_new = jnp.maximum(m_prev, jnp.max(s, axis=-1, keepdims=True))
    alpha = jnp.exp(m_prev - m_new)
    p = jnp.exp(s - m_new)
    l_new = alpha * l_prev + jnp.sum(p, axis=-1, keepdims=True)
    return p, alpha, m_new, l_new


def _prompt_attn_kernel(qs_ref, qd_ref, ks_ref, vs_ref, kd_ref, vd_ref, lam_ref, sub_ref, o_ref,
                        acc_ref, st0_ref, st1_ref, *, tq, lambda_init, n_pairs):
    i = pl.program_id(1)
    rows = 2 * tq
    lane = lax.broadcasted_iota(jnp.int32, (rows, LANES), 1)
    row = lax.broadcasted_iota(jnp.int32, (rows, LANES), 0)
    keep = (lane < 64) == (row < tq)
    qrow = lax.broadcasted_iota(jnp.int32, (rows, tq), 0)
    qidx = jnp.where(qrow < tq, qrow, qrow - tq)
    kidx = lax.broadcasted_iota(jnp.int32, (rows, tq), 1)
    mask_strict = kidx < qidx
    mask_incl = kidx <= qidx
    tri = _strict_lower_ones(tq)
    lam = _diff_lambda(lam_ref, lambda_init)

    def stacked_q(q_ref, p):
        qp = q_ref[0, :, p * LANES:(p + 1) * LANES]
        q2 = jnp.concatenate([qp, qp], axis=0).astype(F32)
        return jnp.where(keep, q2, 0.0).astype(BF16)

    first_chunk = lax.broadcasted_iota(jnp.int32, (tq, LANES), 1) < 64

    def merge(x):
        return jnp.where(first_chunk, x[:tq], x[tq:])

    for p in range(n_pairs):
        q2 = stacked_q(qs_ref, p)
        cols = slice(p * LANES, (p + 1) * LANES)

        def sb_block(j, mask, q2=q2, cols=cols):
            k = ks_ref[0, pl.ds(pl.multiple_of(j * tq, tq), tq), cols]
            v = vs_ref[0, pl.ds(pl.multiple_of(j * tq, tq), tq), cols]
            a, run = _sb_weights(_dot_nt(q2, k), st0_ref[...], tri, mask)
            acc_ref[...] += _dot(a.astype(BF16), v)
            st0_ref[...] = run

        acc_ref[...] = jnp.zeros_like(acc_ref)
        st0_ref[...] = jnp.zeros_like(st0_ref)
        sb_block(i, mask_strict)

        def sb_body(t, carry, sb_block=sb_block):
            sb_block(i - 1 - t, None)
            return carry

        lax.fori_loop(0, i, sb_body, 0)
        o_ref[0, :, cols] = merge(acc_ref[...]).astype(o_ref.dtype)

        q2 = stacked_q(qd_ref, p)

        def diff_block(j, mask, q2=q2, cols=cols):
            k = kd_ref[0, pl.ds(pl.multiple_of(j * tq, tq), tq), cols]
            v = vd_ref[0, pl.ds(pl.multiple_of(j * tq, tq), tq), cols]
            s = _dot_nt(q2, k)
            if mask is not None:
                s = jnp.where(mask, s, NEG)
            pr, alpha, m_new, l_new = _softmax_step(s, st0_ref[...], st1_ref[...])
            acc_ref[...] = alpha * acc_ref[...] + _dot(pr.astype(BF16), v)
            st0_ref[...] = m_new
            st1_ref[...] = l_new

        acc_ref[...] = jnp.zeros_like(acc_ref)
        st0_ref[...] = jnp.full_like(st0_ref, NEG)
        st1_ref[...] = jnp.zeros_like(st1_ref)
        diff_block(i, mask_incl)

        def diff_body(t, carry, diff_block=diff_block):
            diff_block(i - 1 - t, None)
            return carry

        lax.fori_loop(0, i, diff_body, 0)
        w = acc_ref[...] / st1_ref[...]
        od = w[:tq] - lam * w[tq:]
        od = _rms(od, sub_ref[...]) * (1.0 - lambda_init)
        dcols = slice((n_pairs + p) * LANES, (n_pairs + p + 1) * LANES)
        o_ref[0, :, dcols] = od.astype(o_ref.dtype)


def _prompt_attention(qs, qd, ks, vs, kd, vd, lam_vecs, subln, lambda_init):
    b, t, w = qs.shape
    tq = ATTN_TILE
    n_pairs = w // LANES
    assert t % tq == 0 and w % LANES == 0
    qspec = pl.BlockSpec((1, tq, w), lambda bi, i: (bi, i, 0))
    kvspec = pl.BlockSpec((1, t, w), lambda bi, i: (bi, 0, 0))
    return pl.pallas_call(
        functools.partial(_prompt_attn_kernel, tq=tq, lambda_init=lambda_init, n_pairs=n_pairs),
        grid=(b, t // tq),
        in_specs=[qspec, qspec, kvspec, kvspec, kvspec, kvspec,
                  _resident(lam_vecs.shape), _resident(subln.shape)],
        out_specs=pl.BlockSpec((1, tq, 2 * w), lambda bi, i: (bi, i, 0)),
        out_shape=jax.ShapeDtypeStruct((b, t, 2 * w), BF16),
        scratch_shapes=[pltpu.VMEM((2 * tq, LANES), F32), pltpu.VMEM((2 * tq, 1), F32),
                        pltpu.VMEM((2 * tq, 1), F32)],
        compiler_params=_params(("parallel", "arbitrary")), name="prompt_attention",
    )(qs, qd, ks, vs, kd, vd, lam_vecs, subln)


def _sample_attn_kernel(pt_ref, *refs, pps, ts, n_chunks, page, lambda_init):
    del pt_ref
    caches = [refs[c * pps:(c + 1) * pps] for c in range(4)]
    (qs_ref, qd_ref, nks_ref, nvs_ref, nkd_ref, nvd_ref, lam_ref, sub_ref, o_ref,
     acc_s_ref, run_ref, acc_d_ref, m_ref, l_ref) = refs[4 * pps:]
    c = pl.program_id(1)
    rows, w = acc_s_ref.shape
    n_chunk_heads = rows // ts
    tri = _strict_lower_ones(MXU_DIM)

    r_w = lax.broadcasted_iota(jnp.int32, (rows, w), 0)
    c_w = lax.broadcasted_iota(jnp.int32, (rows, w), 1)
    chunk_of_row = r_w % n_chunk_heads
    q_keep = (c_w // 64) == chunk_of_row

    def block_diag_q(q_ref):
        return jnp.where(q_keep, q_ref[0].astype(F32), 0.0).astype(BF16)

    q_sb = block_diag_q(qs_ref)
    q_d = block_diag_q(qd_ref)

    def sb_chunk(k, v, mask, tri_blk):
        a, run = _sb_weights(_dot_nt(q_sb, k), run_ref[...], tri_blk, mask)
        acc_s_ref[...] += _dot(a.astype(BF16), v)
        run_ref[...] = run

    def diff_chunk(k, v, mask):
        s = _dot_nt(q_d, k)
        if mask is not None:
            s = jnp.where(mask, s, NEG)
        pr, alpha, m_new, l_new = _softmax_step(s, m_ref[...], l_ref[...])
        acc_d_ref[...] = alpha * acc_d_ref[...] + _dot(pr.astype(BF16), v)
        m_ref[...] = m_new
        l_ref[...] = l_new

    @pl.when(c == 0)
    def _():
        acc_s_ref[...] = jnp.zeros_like(acc_s_ref)
        run_ref[...] = jnp.zeros_like(run_ref)
        acc_d_ref[...] = jnp.zeros_like(acc_d_ref)
        m_ref[...] = jnp.full_like(m_ref, NEG)
        l_ref[...] = jnp.zeros_like(l_ref)
        def padded(ref):
            x = ref[0]
            pad = jnp.zeros((LANES - x.shape[0], x.shape[1]), x.dtype)
            return jnp.concatenate([x, pad], axis=0).astype(BF16)
        key = lax.broadcasted_iota(jnp.int32, (rows, LANES), 1)
        qpos = lax.broadcasted_iota(jnp.int32, (rows, LANES), 0) // n_chunk_heads
        sb_chunk(padded(nks_ref), padded(nvs_ref), key < qpos, tri[:LANES, :LANES])
        diff_chunk(padded(nkd_ref), padded(nvd_ref), key <= qpos)

    def gathered(page_refs):
        return jnp.concatenate([r[...].astype(BF16) for r in page_refs], axis=0)

    sb_chunk(gathered(caches[0]), gathered(caches[1]), None, tri)
    diff_chunk(gathered(caches[2]), gathered(caches[3]), None)

    @pl.when(c == n_chunks - 1)
    def _():
        lam = _diff_lambda(lam_ref, lambda_init)
        o_sb = jnp.where(q_keep, acc_s_ref[...], 0.0)
        comp = chunk_of_row % 2
        coef = jnp.where(comp == 0, 1.0, -lam) / l_ref[...]
        d_keep = (c_w // LANES) == (chunk_of_row // 2)
        o_d = jnp.where(d_keep, acc_d_ref[...] * coef, 0.0)

        def per_query(x):
            return jnp.concatenate(
                [jnp.sum(x[q * n_chunk_heads:(q + 1) * n_chunk_heads], axis=0, keepdims=True)
                 for q in range(ts)], axis=0)

        o_sb = per_query(o_sb)
        o_d = per_query(o_d)
        normed = [_rms(o_d[:, h * LANES:(h + 1) * LANES], sub_ref[...]) * (1.0 - lambda_init)
                  for h in range(w // LANES)]
        o_ref[0] = jnp.concatenate([o_sb] + normed, axis=1).astype(o_ref.dtype)


def _sample_attention(page_table, caches, layer, qs_exp, qd_exp, new_kv, lam_vecs, subln, lambda_init):
    bs, n_pages = page_table.shape
    page, w = caches[0].shape[2], caches[0].shape[3]
    rows = qs_exp.shape[1]
    ts = rows // (w // 64)
    pps = PAGES_PER_STEP
    assert n_pages % pps == 0 and (pps * page) % MXU_DIM == 0
    n_chunks = n_pages // pps
    pt_flat = page_table.reshape(-1)

    def page_spec(i):
        def imap(b, c, pt):
            return (layer, pt[b * n_pages + (n_chunks - 1 - c) * pps + i], 0, 0)
        return pl.BlockSpec((None, None, page, w), imap)

    in_specs, args = [], []
    for cache in caches:
        for i in range(pps):
            in_specs.append(page_spec(i))
            args.append(cache)
    per_b = lambda shape: pl.BlockSpec((1,) + shape, lambda b, c, pt: (b, 0, 0))
    in_specs += [per_b((rows, w)), per_b((rows, w))] + [per_b(new_kv[0].shape[1:])] * 4
    args += [qs_exp, qd_exp] + list(new_kv)
    in_specs += [pl.BlockSpec(lam_vecs.shape, lambda b, c, pt: (0, 0)),
                 pl.BlockSpec(subln.shape, lambda b, c, pt: (0, 0))]
    args += [lam_vecs, subln]
    grid_spec = pltpu.PrefetchScalarGridSpec(
        num_scalar_prefetch=1, grid=(bs, n_chunks), in_specs=in_specs,
        out_specs=pl.BlockSpec((1, ts, 2 * w), lambda b, c, pt: (b, 0, 0)),
        scratch_shapes=[pltpu.VMEM((rows, w), F32), pltpu.VMEM((rows, 1), F32),
                        pltpu.VMEM((rows, w), F32), pltpu.VMEM((rows, 1), F32),
                        pltpu.VMEM((rows, 1), F32)])
    return pl.pallas_call(
        functools.partial(_sample_attn_kernel, pps=pps, ts=ts, n_chunks=n_chunks, page=page,
                          lambda_init=lambda_init),
        grid_spec=grid_spec, out_shape=jax.ShapeDtypeStruct((bs, ts, 2 * w), BF16),
        compiler_params=_params(("parallel", "arbitrary")), name="sample_attention",
    )(pt_flat, *args)


def _mem_attn_kernel(q_ref, mk_ref, mv_ref, o_ref, *, n_heads):
    hd = q_ref.shape[2] // n_heads
    for h in range(n_heads):
        cols = slice(h * hd, (h + 1) * hd)
        s = _dot_nt(q_ref[0, :, cols], mk_ref[0, :, cols].astype(BF16))
        p = jnp.exp(s - jnp.max(s, axis=-1, keepdims=True))
        l = jnp.sum(p, axis=-1, keepdims=True)
        o = _dot(p.astype(BF16), mv_ref[0, :, cols].astype(BF16)) / l
        o_ref[0, :, cols] = o.astype(o_ref.dtype)


def _mem_attention(q, mk, mv, n_heads):
    b, t, w = q.shape
    tm = min(TOKEN_TILE, t)
    assert t % tm == 0
    m = mk.shape[1]
    return pl.pallas_call(
        functools.partial(_mem_attn_kernel, n_heads=n_heads), grid=(b, t // tm),
        in_specs=[pl.BlockSpec((1, tm, w), lambda bi, i: (bi, i, 0)),
                  pl.BlockSpec((1, m, w), lambda bi, i: (bi, 0, 0)),
                  pl.BlockSpec((1, m, w), lambda bi, i: (bi, 0, 0))],
        out_specs=pl.BlockSpec((1, tm, w), lambda bi, i: (bi, i, 0)),
        out_shape=jax.ShapeDtypeStruct((b, t, w), BF16),
        compiler_params=_params(("parallel", "arbitrary")), name="mem_attention",
    )(q, mk, mv)


def _ffn_kernel(h_ref, cin_ref, g_ref, wup_ref, wconv_ref, bconv_ref, wdown_ref, gfin_ref,
                y_ref, cout_ref, ubuf_g_ref, ubuf_v_ref, act_ref, *, shift, dff, final_norm):
    t = pl.program_id(1)
    tm = h_ref.shape[1]
    cr = cout_ref.shape[1]

    @pl.when(t == 0)
    def _():
        cout_ref[0] = cin_ref[0]

    h = h_ref[0]
    hn = _rms(h, g_ref[...]).astype(BF16)

    def conv(ubuf_ref, c0):
        cols = slice(c0, c0 + FF_CHUNK)
        u = _dot(hn, wup_ref[:, cols])
        ubuf_ref[0:cr, :] = cout_ref[0, :, cols]
        ubuf_ref[cr:cr + tm, :] = u
        out = bconv_ref[:, cols] + u * wconv_ref[CONV_W - 1:CONV_W, cols]
        for i in range(CONV_W - 1):
            back = (CONV_W - 1 - i) * shift
            out = out + ubuf_ref[cr - back:cr - back + tm, :] * wconv_ref[i:i + 1, cols]
        cout_ref[0, :, cols] = ubuf_ref[tm:tm + cr, :]
        return out

    for j in range(dff // FF_CHUNK):
        gate = conv(ubuf_g_ref, j * FF_CHUNK)
        val = conv(ubuf_v_ref, dff + j * FF_CHUNK)
        act = gate * (1.0 / (1.0 + jnp.exp(-gate))) * val
        act_ref[:, j * FF_CHUNK:(j + 1) * FF_CHUNK] = act.astype(BF16)

    out = h + _dot(act_ref[...], wdown_ref[...])
    y_ref[0] = _rms(out, gfin_ref[...]) if final_norm else out


def _conv_ffn(h, conv_in, g, w_up, w_conv, b_conv, w_down, g_final, final_norm, shift):
    ngrp, t, d = h.shape
    dff = w_down.shape[0]
    cr = conv_in.shape[1]
    tm = min(TOKEN_TILE, t)
    assert t % tm == 0 and dff % FF_CHUNK == 0 and cr >= (CONV_W - 1) * shift and tm >= cr
    return pl.pallas_call(
        functools.partial(_ffn_kernel, shift=shift, dff=dff, final_norm=final_norm),
        grid=(ngrp, t // tm),
        in_specs=[pl.BlockSpec((1, tm, d), lambda gi, i: (gi, i, 0)),
                  pl.BlockSpec((1, cr, 2 * dff), lambda gi, i: (gi, 0, 0),
                               pipeline_mode=pl.Buffered(1)),
                  _resident((1, d)), _resident(w_up.shape), _resident(w_conv.shape),
                  _resident((1, 2 * dff)), _resident(w_down.shape), _resident((1, d))],
        out_specs=[pl.BlockSpec((1, tm, d), lambda gi, i: (gi, i, 0)),
                   pl.BlockSpec((1, cr, 2 * dff), lambda gi, i: (gi, 0, 0))],
        out_shape=[jax.ShapeDtypeStruct((ngrp, t, d), F32),
                   jax.ShapeDtypeStruct((ngrp, cr, 2 * dff), F32)],
        scratch_shapes=[pltpu.VMEM((cr + tm, FF_CHUNK), F32), pltpu.VMEM((cr + tm, FF_CHUNK), F32),
                        pltpu.VMEM((tm, dff), BF16)],
        compiler_params=_params(("parallel", "arbitrary")), name="conv_ffn",
    )(h, conv_in, g.reshape(1, d), w_up, w_conv, b_conv.reshape(1, 2 * dff), w_down,
      g_final.reshape(1, d))


def kernel(x_prompt, x_sample, cache_sb_k, cache_sb_v, cache_diff_k, cache_diff_v, cache_mem_k, cache_mem_v, state_conv, page_table, mem_prompt, norm_mix, w_in, lambda_q1, lambda_k1, lambda_q2, lambda_k2, diff_subln, w_out, norm_mem_q, norm_mem_kv, w_mem_q, w_mem_k, w_mem_v, w_mem_o, norm_ffn, w_up, w_conv, b_conv, w_down, norm_final):
    bp, tp, d = x_prompt.shape
    bs, ts, _ = x_sample.shape
    depth, n_pool, page, sb_heads, sb_hd = cache_sb_k.shape
    diff_heads, diff_hd = cache_diff_k.shape[3], cache_diff_k.shape[5]
    mem_tokens, mem_heads, mem_hd = cache_mem_k.shape[2:]
    n_pages = page_table.shape[1]
    past_len = n_pages * page
    sb_w = sb_heads * sb_hd
    dqk_w = diff_heads * 2 * diff_hd
    dv_w = diff_heads * 2 * diff_hd
    mem_w = mem_heads * mem_hd
    dff = w_down.shape[1]
    assert sb_hd == 64 and diff_hd == 64 and sb_w == dqk_w == dv_w and ts <= SUBLANES
    w = sb_w
    n_chunk_heads = w // 64

    sb_scale = 1.0 / math.sqrt(sb_hd)
    diff_scale = 1.0 / math.sqrt(diff_hd)
    mem_scale = 1.0 / math.sqrt(mem_hd)
    assert math.log2(sb_scale).is_integer() and math.log2(mem_scale).is_integer()

    in_segs = [(0, w, sb_scale, False, False, True), (w, w, 1.0, False, True, True),
               (2 * w, w, 1.0, False, True, True), (3 * w, w, diff_scale, True, False, True),
               (4 * w, w, 1.0, True, True, True), (5 * w, w, 1.0, False, True, True)]
    rope_p = _rope_table(jnp.arange(tp, dtype=jnp.int32))
    rope_s = jnp.tile(_rope_table(past_len + jnp.arange(ts, dtype=jnp.int32)), (bs, 1))

    hp = x_prompt.reshape(bp * tp, d)
    hs = x_sample.reshape(bs * ts, d)
    outs = {k: [] for k in ("p_sbk", "p_sbv", "p_dk", "p_dv", "p_mk", "p_mv", "p_conv",
                            "s_sbk", "s_sbv", "s_dk", "s_dv", "s_conv")}
    conv_rows_p = SUBLANES
    for l in range(depth):
        lambda_init = 0.8 - 0.6 * math.exp(-0.3 * l)
        lam_vecs = jnp.stack([lambda_q1[l], lambda_k1[l], lambda_q2[l], lambda_k2[l]]).astype(F32)
        subln = diff_subln[l].reshape(1, -1)
        w_in_b = w_in[l].astype(BF16)
        w_out_b = w_out[l].astype(BF16)
        w_q_b = w_mem_q[l].astype(BF16)
        w_kv_b = jnp.concatenate([w_mem_k[l], w_mem_v[l]], axis=1).astype(BF16)
        w_o_b = w_mem_o[l].astype(BF16)
        w_up_b = w_up[l].astype(BF16)
        w_down_b = w_down[l].astype(BF16)
        q_seg = [(0, mem_w, mem_scale, False, False, True)]

        def mem_block(h2d, ngrp, mk, mv):
            (q,) = _norm_matmul(h2d, norm_mem_q[l], w_q_b, q_seg)
            t = h2d.shape[0] // ngrp
            o = _mem_attention(q.reshape(ngrp, t, mem_w), mk, mv, mem_heads)
            return _matmul_residual(o.reshape(ngrp * t, mem_w), w_o_b, h2d)

        qs, ksf, ksb, vsf, vsb, qd, kdf, kdb, vdf, vdb = _norm_matmul(
            hp, norm_mix[l], w_in_b, in_segs, rope_p)
        r3 = lambda a: a.reshape(bp, tp, w)
        o = _prompt_attention(r3(qs), r3(qd), r3(ksb), r3(vsb), r3(kdb), r3(vdb), lam_vecs, subln,
                              lambda_init)
        hp = _matmul_residual(o.reshape(bp * tp, 2 * w), w_out_b, hp)
        mkf, mvf = _norm_matmul(mem_prompt.reshape(bp * mem_tokens, d), norm_mem_kv[l], w_kv_b,
                                [(0, mem_w, 1.0, False, True, False),
                                 (mem_w, mem_w, 1.0, False, True, False)])
        mk3 = mkf.reshape(bp, mem_tokens, mem_w)
        mv3 = mvf.reshape(bp, mem_tokens, mem_w)
        hp = mem_block(hp, bp, mk3, mv3)
        conv0 = jnp.zeros((bp, conv_rows_p, 2 * dff), F32)
        last = l == depth - 1
        yp, conv_p = _conv_ffn(hp.reshape(bp, tp, d), conv0, norm_ffn[l], w_up_b, w_conv[l],
                               b_conv[l], w_down_b, norm_final, last, 1)
        outs["p_sbk"].append(ksf.reshape(bp, tp, sb_heads, sb_hd))
        outs["p_sbv"].append(vsf.reshape(bp, tp, sb_heads, sb_hd))
        outs["p_dk"].append(kdf.reshape(bp, tp, diff_heads, 2, diff_hd))
        outs["p_dv"].append(vdf.reshape(bp, tp, diff_heads, 2 * diff_hd))
        outs["p_mk"].append(mk3.reshape(bp, mem_tokens, mem_heads, mem_hd))
        outs["p_mv"].append(mv3.reshape(bp, mem_tokens, mem_heads, mem_hd))
        outs["p_conv"].append(conv_p[:, conv_rows_p - (CONV_W - 1):])

        qs, ksf, _, vsf, _, qd, kdf, _, vdf, _ = _norm_matmul(hs, norm_mix[l], w_in_b, in_segs, rope_s)
        expand = lambda q: jnp.repeat(q.reshape(bs, ts, w), n_chunk_heads, axis=1)
        pad_new = lambda a: jnp.pad(a.reshape(bs, ts, w), ((0, 0), (0, SUBLANES - ts), (0, 0)))
        caches = [c.reshape(depth, n_pool, page, w) for c in
                  (cache_sb_k, cache_sb_v, cache_diff_k, cache_diff_v)]
        o = _sample_attention(page_table, caches, l, expand(qs), expand(qd),
                              [pad_new(a) for a in (ksf, vsf, kdf, vdf)], lam_vecs, subln, lambda_init)
        hs = _matmul_residual(o.reshape(bs * ts, 2 * w), w_out_b, hs)
        hs = mem_block(hs, bs, cache_mem_k[l].reshape(bs, mem_tokens, mem_w),
                       cache_mem_v[l].reshape(bs, mem_tokens, mem_w))
        hs_tm = hs.reshape(bs, ts, d).transpose(1, 0, 2).reshape(1, ts * bs, d)
        conv_in = state_conv[l].transpose(1, 0, 2).reshape(1, (CONV_W - 1) * bs, 2 * dff)
        ys, conv_s = _conv_ffn(hs_tm, conv_in, norm_ffn[l], w_up_b, w_conv[l], b_conv[l], w_down_b,
                               norm_final, last, bs)
        hs = ys.reshape(ts, bs, d).transpose(1, 0, 2).reshape(bs * ts, d)
        outs["s_sbk"].append(ksf.reshape(bs, ts, sb_heads, sb_hd))
        outs["s_sbv"].append(vsf.reshape(bs, ts, sb_heads, sb_hd))
        outs["s_dk"].append(kdf.reshape(bs, ts, diff_heads, 2, diff_hd))
        outs["s_dv"].append(vdf.reshape(bs, ts, diff_heads, 2 * diff_hd))
        outs["s_conv"].append(conv_s.reshape(CONV_W - 1, bs, 2 * dff).transpose(1, 0, 2))
        hp = yp.reshape(bp * tp, d)

    st = lambda xs: jnp.stack(xs, axis=0)
    return (hp.reshape(bp, tp, d), hs.reshape(bs, ts, d), st(outs["p_sbk"]), st(outs["p_sbv"]),
            st(outs["p_dk"]), st(outs["p_dv"]), st(outs["p_mk"]), st(outs["p_mv"]), st(outs["p_conv"]),
            st(outs["s_sbk"]), st(outs["s_sbv"]), st(outs["s_dk"]), st(outs["s_dv"]), st(outs["s_conv"]))
```

```python
import functools
import math

import jax
import jax.numpy as jnp
from jax import lax
from jax.experimental import pallas as pl
from jax.experimental.pallas import tpu as pltpu

F32 = jnp.float32
BF16 = jnp.bfloat16

EPS = 1e-6
ROPE_DIM = 16
ROPE_THETA = 500000.0
CONV_W = 3
NEG = -1e30

LANES = 128
SUBLANES = 8
MXU_DIM = 256
VMEM_LIMIT = 56 * 1024 * 1024

TOKEN_TILE = 512
ATTN_TILE = 256
PAGES_PER_STEP = 16
FF_CHUNK = 256


def _dot(a, b):
    return jnp.dot(a, b, preferred_element_type=F32)


def _dot_nt(a, b):
    return lax.dot_general(a, b, (((1,), (1,)), ((), ())), preferred_element_type=F32)


def _rms(x, g):
    return x * lax.rsqrt(jnp.mean(x * x, axis=-1, keepdims=True) + EPS) * g


def _resident(shape):
    nd = len(shape)
    return pl.BlockSpec(shape, lambda *_: (0,) * nd, pipeline_mode=pl.Buffered(1))


def _params(sem):
    return pltpu.CompilerParams(dimension_semantics=sem, vmem_limit_bytes=VMEM_LIMIT)


def _split_hi_lo(x):
    hi = x.astype(BF16)
    lo = (x - hi.astype(F32)).astype(BF16)
    return jnp.concatenate([hi, lo], axis=1)


def _suffix_sum_matrix(n):
    r = lax.broadcasted_iota(jnp.int32, (2 * n, n), 0)
    c = lax.broadcasted_iota(jnp.int32, (2 * n, n), 1)
    return jnp.where(jnp.where(r >= n, r - n, r) > c, 1.0, 0.0).astype(BF16)


def _neg_softplus(s):
    return -(jnp.maximum(s, 0.0) + jnp.log(1.0 + jnp.exp(-jnp.abs(s))))


def _norm_matmul_kernel(*refs, segs, has_rope):
    x_ref, g_ref, w_ref = refs[:3]
    rope_ref = refs[3] if has_rope else None
    outs = refs[4:] if has_rope else refs[3:]
    xn = _rms(x_ref[...], g_ref[...]).astype(BF16)
    oi = 0
    for c0, width, scale, rope, want_f32, want_bf16 in segs:
        u = _dot(xn, w_ref[:, c0:c0 + width])
        if rope:
            cos = rope_ref[:, 0:LANES]
            sin_lo = rope_ref[:, LANES:2 * LANES]
            sin_hi = rope_ref[:, 2 * LANES:3 * LANES]
            half = ROPE_DIM // 2
            pieces = []
            for j in range(width // LANES):
                uj = u[:, j * LANES:(j + 1) * LANES]
                pieces.append(uj * cos + pltpu.roll(uj, LANES - half, 1) * sin_lo
                              + pltpu.roll(uj, half, 1) * sin_hi)
            u = jnp.concatenate(pieces, axis=1)
        if want_f32:
            outs[oi][...] = u
            oi += 1
        if want_bf16:
            outs[oi][...] = (u * scale).astype(BF16)
            oi += 1


def _norm_matmul(x, g, w_bf16, segs, rope_tab=None):
    m, d = x.shape
    tm = min(TOKEN_TILE, m)
    assert m % tm == 0
    in_specs = [pl.BlockSpec((tm, d), lambda i: (i, 0)), _resident((1, d)), _resident(w_bf16.shape)]
    args = [x, g.reshape(1, d), w_bf16]
    if rope_tab is not None:
        nrt = rope_tab.shape[0] // tm
        assert rope_tab.shape[0] % tm == 0
        in_specs.append(pl.BlockSpec((tm, 3 * LANES), lambda i: (i % nrt, 0)))
        args.append(rope_tab)
    out_shape, out_specs = [], []
    for _, width, _, _, want_f32, want_bf16 in segs:
        for want, dt in ((want_f32, F32), (want_bf16, BF16)):
            if want:
                out_shape.append(jax.ShapeDtypeStruct((m, width), dt))
                out_specs.append(pl.BlockSpec((tm, width), lambda i: (i, 0)))
    return pl.pallas_call(
        functools.partial(_norm_matmul_kernel, segs=tuple(segs), has_rope=rope_tab is not None),
        grid=(m // tm,), in_specs=in_specs, out_specs=out_specs, out_shape=out_shape,
        compiler_params=_params(("parallel",)), name="norm_matmul",
    )(*args)


def _rope_table(pos):
    half = ROPE_DIM // 2
    inv_freq = ROPE_THETA ** (-jnp.arange(half, dtype=F32) * 2.0 / ROPE_DIM)
    ang = pos.astype(F32)[:, None] * inv_freq[None, :]
    cos, sin = jnp.cos(ang), jnp.sin(ang)
    n = pos.shape[0]
    ones = jnp.ones((n, 64 - ROPE_DIM), F32)
    zeros = jnp.zeros((n, 64 - ROPE_DIM), F32)
    zh = jnp.zeros((n, half), F32)
    c64 = jnp.concatenate([cos, cos, ones], axis=1)
    lo64 = jnp.concatenate([-sin, zh, zeros], axis=1)
    hi64 = jnp.concatenate([zh, sin, zeros], axis=1)
    rep = LANES // 64
    return jnp.concatenate([jnp.tile(c64, (1, rep)), jnp.tile(lo64, (1, rep)), jnp.tile(hi64, (1, rep))], axis=1)


def _matmul_residual_kernel(a_ref, w_ref, r_ref, o_ref):
    o_ref[...] = r_ref[...] + _dot(a_ref[...].astype(BF16), w_ref[...])


def _matmul_residual(a, w_bf16, res):
    m, k = a.shape
    n = w_bf16.shape[1]
    tm = min(TOKEN_TILE, m)
    assert m % tm == 0
    return pl.pallas_call(
        _matmul_residual_kernel, grid=(m // tm,),
        in_specs=[pl.BlockSpec((tm, k), lambda i: (i, 0)), _resident(w_bf16.shape),
                  pl.BlockSpec((tm, n), lambda i: (i, 0))],
        out_specs=pl.BlockSpec((tm, n), lambda i: (i, 0)),
        out_shape=jax.ShapeDtypeStruct((m, n), F32),
        compiler_params=_params(("parallel",)), name="matmul_residual",
    )(a, w_bf16, res)


def _diff_lambda(lam_ref, lambda_init):
    a = jnp.sum(lam_ref[0:1, :] * lam_ref[1:2, :], axis=-1, keepdims=True)
    b = jnp.sum(lam_ref[2:3, :] * lam_ref[3:4, :], axis=-1, keepdims=True)
    return jnp.exp(a) - jnp.exp(b) + lambda_init


def _sb_weights(s, run, tri, mask):
    n = s.shape[1]
    blk = tri.shape[1]
    nb = n // blk
    log1m = _neg_softplus(s)
    log_sig = s + log1m
    if mask is not None:
        log1m = jnp.where(mask, log1m, 0.0)
    stacked = jnp.concatenate([log1m[:, b * blk:(b + 1) * blk] for b in range(nb)], axis=0)
    loc = _dot(_split_hi_lo(stacked), tri)
    rows = s.shape[0]
    after = []
    for b in reversed(range(nb)):
        loc_b = loc[b * rows:(b + 1) * rows]
        after.append(loc_b + run)
        run = run + loc_b[:, 0:1] + log1m[:, b * blk:b * blk + 1]
    after = jnp.concatenate(after[::-1], axis=1) if nb > 1 else after[0]
    a = jnp.exp(log_sig + after)
    if mask is not None:
        a = jnp.where(mask, a, 0.0)
    return a, run


def _softmax_step(s, m_prev, l_prev):
    m_new = jnp.maximum(m_prev, jnp.max(s, axis=-1, keepdims=True))
    alpha = jnp.exp(m_prev - m_new)
    p = jnp.exp(s - m_new)
    l_new = alpha * l_prev + jnp.sum(p, axis=-1, keepdims=True)
    return p, alpha, m_new, l_new


def _prompt_attn_kernel(qs_ref, qd_ref, ks_ref, vs_ref, kd_ref, vd_ref, lam_ref, sub_ref, o_ref,
                        q2_ref, acc_ref, st0_ref, st1_ref, *, tq, lambda_init, n_pairs):
    i = pl.program_id(1)
    rows = 2 * tq
    lane = lax.broadcasted_iota(jnp.int32, (rows, LANES), 1)
    row = lax.broadcasted_iota(jnp.int32, (rows, LANES), 0)
    keep = (lane < 64) == (row < tq)
    qrow = lax.broadcasted_iota(jnp.int32, (rows, tq), 0)
    qidx = jnp.where(qrow < tq, qrow, qrow - tq)
    kidx = lax.broadcasted_iota(jnp.int32, (rows, tq), 1)
    mask_strict = kidx < qidx
    mask_incl = kidx <= qidx
    tri = _suffix_sum_matrix(tq)
    lam = _diff_lambda(lam_ref, lambda_init)

    for p in range(n_pairs):
        for c, q_ref in ((p, qs_ref), (n_pairs + p, qd_ref)):
            qp = q_ref[0, :, p * LANES:(p + 1) * LANES]
            q2 = jnp.concatenate([qp, qp], axis=0).astype(F32)
            q2_ref[c] = jnp.where(keep, q2, 0.0).astype(BF16)
    acc_ref[...] = jnp.zeros_like(acc_ref)
    st0_ref[0:n_pairs] = jnp.zeros((n_pairs, rows, 1), F32)
    st0_ref[n_pairs:2 * n_pairs] = jnp.full((n_pairs, rows, 1), NEG, F32)
    st1_ref[...] = jnp.zeros_like(st1_ref)

    def key_block(j, mask_sb, mask_diff):
        krows = pl.ds(pl.multiple_of(j * tq, tq), tq)
        for p in range(n_pairs):
            cols = slice(p * LANES, (p + 1) * LANES)
            a, run = _sb_weights(_dot_nt(q2_ref[p], ks_ref[0, krows, cols]), st0_ref[p], tri, mask_sb)
            acc_ref[p] += _dot(a.astype(BF16), vs_ref[0, krows, cols])
            st0_ref[p] = run
            c = n_pairs + p
            s = _dot_nt(q2_ref[c], kd_ref[0, krows, cols])
            if mask_diff is not None:
                s = jnp.where(mask_diff, s, NEG)
            pr, alpha, m_new, l_new = _softmax_step(s, st0_ref[c], st1_ref[p])
            acc_ref[c] = alpha * acc_ref[c] + _dot(pr.astype(BF16), vd_ref[0, krows, cols])
            st0_ref[c] = m_new
            st1_ref[p] = l_new

    key_block(i, mask_strict, mask_incl)

    def below_diagonal(t, carry):
        key_block(i - 1 - t, None, None)
        return carry

    lax.fori_loop(0, i, below_diagonal, 0)

    first_chunk = lax.broadcasted_iota(jnp.int32, (tq, LANES), 1) < 64
    for p in range(n_pairs):
        acc = acc_ref[p]
        o_ref[0, :, p * LANES:(p + 1) * LANES] = jnp.where(first_chunk, acc[:tq], acc[tq:]).astype(o_ref.dtype)
        w = acc_ref[n_pairs + p] / st1_ref[p]
        od = _rms(w[:tq] - lam * w[tq:], sub_ref[...]) * (1.0 - lambda_init)
        o_ref[0, :, (n_pairs + p) * LANES:(n_pairs + p + 1) * LANES] = od.astype(o_ref.dtype)


def _prompt_attention(qs, qd, ks, vs, kd, vd, lam_vecs, subln, lambda_init):
    b, t, w = qs.shape
    tq = ATTN_TILE
    n_pairs = w // LANES
    assert t % tq == 0 and w % LANES == 0
    qspec = pl.BlockSpec((1, tq, w), lambda bi, i: (bi, i, 0))
    kvspec = pl.BlockSpec((1, t, w), lambda bi, i: (bi, 0, 0))
    return pl.pallas_call(
        functools.partial(_prompt_attn_kernel, tq=tq, lambda_init=lambda_init, n_pairs=n_pairs),
        grid=(b, t // tq),
        in_specs=[qspec, qspec, kvspec, kvspec, kvspec, kvspec,
                  _resident(lam_vecs.shape), _resident(subln.shape)],
        out_specs=pl.BlockSpec((1, tq, 2 * w), lambda bi, i: (bi, i, 0)),
        out_shape=jax.ShapeDtypeStruct((b, t, 2 * w), BF16),
        scratch_shapes=[pltpu.VMEM((2 * n_pairs, 2 * tq, LANES), BF16),
                        pltpu.VMEM((2 * n_pairs, 2 * tq, LANES), F32),
                        pltpu.VMEM((2 * n_pairs, 2 * tq, 1), F32),
                        pltpu.VMEM((n_pairs, 2 * tq, 1), F32)],
        compiler_params=_params(("parallel", "arbitrary")), name="prompt_attention",
    )(qs, qd, ks, vs, kd, vd, lam_vecs, subln)


def _sample_attn_kernel(pt_ref, *refs, pps, ts, n_chunks, page, lambda_init):
    del pt_ref
    caches = [refs[c * pps:(c + 1) * pps] for c in range(4)]
    (qs_ref, qd_ref, nks_ref, nvs_ref, nkd_ref, nvd_ref, lam_ref, sub_ref, o_ref,
     acc_s_ref, run_ref, acc_d_ref, m_ref, l_ref) = refs[4 * pps:]
    c = pl.program_id(1)
    rows, w = acc_s_ref.shape
    n_chunk_heads = rows // ts
    tri = _suffix_sum_matrix(MXU_DIM)

    r_w = lax.broadcasted_iota(jnp.int32, (rows, w), 0)
    c_w = lax.broadcasted_iota(jnp.int32, (rows, w), 1)
    chunk_of_row = r_w % n_chunk_heads
    q_keep = (c_w // 64) == chunk_of_row

    def block_diag_q(q_ref):
        return jnp.where(q_keep, q_ref[0].astype(F32), 0.0).astype(BF16)

    q_sb = block_diag_q(qs_ref)
    q_d = block_diag_q(qd_ref)

    def sb_chunk(k, v, mask, tri_blk, feature_major):
        s = _dot(q_sb, k) if feature_major else _dot_nt(q_sb, k)
        a, run = _sb_weights(s, run_ref[...], tri_blk, mask)
        a = a.astype(BF16)
        acc_s_ref[...] += _dot_nt(a, v) if feature_major else _dot(a, v)
        run_ref[...] = run

    n_dh = w // LANES
    t_i = lax.broadcasted_iota(jnp.int32, (page, n_dh * page), 0)
    j_i = lax.broadcasted_iota(jnp.int32, (page, n_dh * page), 1)
    spread = jnp.where(j_i // n_dh == t_i, 1.0, 0.0).astype(BF16)
    own_head = (lax.broadcasted_iota(jnp.int32, (rows, n_dh * page), 1) % n_dh
                == (lax.broadcasted_iota(jnp.int32, (rows, n_dh * page), 0) % n_chunk_heads) // 2)

    def diff_chunk(k, v_rows, mask, feature_major):
        s = _dot(q_d, k) if feature_major else _dot_nt(q_d, k)
        if mask is not None:
            s = jnp.where(mask, s, NEG)
        pr, alpha, m_new, l_new = _softmax_step(s, m_ref[...], l_ref[...])
        pr = pr.astype(BF16)
        n_pg = pr.shape[1] // page
        stacked = jnp.concatenate([pr[:, i * page:(i + 1) * page] for i in range(n_pg)], axis=0)
        wide = _dot(stacked, spread)
        wide = jnp.concatenate(
            [jnp.where(own_head, wide[i * rows:(i + 1) * rows], 0.0) for i in range(n_pg)], axis=1)
        acc_d_ref[...] = alpha * acc_d_ref[...] + _dot(wide.astype(BF16), v_rows)
        m_ref[...] = m_new
        l_ref[...] = l_new

    @pl.when(c == 0)
    def _():
        acc_s_ref[...] = jnp.zeros_like(acc_s_ref)
        run_ref[...] = jnp.zeros_like(run_ref)
        acc_d_ref[...] = jnp.zeros_like(acc_d_ref)
        m_ref[...] = jnp.full_like(m_ref, NEG)
        l_ref[...] = jnp.zeros_like(l_ref)
        def padded(ref, n_rows):
            x = ref[0]
            pad = jnp.zeros((n_rows - x.shape[0], x.shape[1]), x.dtype)
            return jnp.concatenate([x, pad], axis=0).astype(BF16)
        key = lax.broadcasted_iota(jnp.int32, (rows, page), 1)
        qpos = lax.broadcasted_iota(jnp.int32, (rows, page), 0) // n_chunk_heads
        sb_chunk(padded(nks_ref, page), padded(nvs_ref, page), key < qpos, _suffix_sum_matrix(page), False)
        diff_chunk(padded(nkd_ref, page), padded(nvd_ref, n_dh * page), key <= qpos, False)

    def gathered(page_refs, axis):
        return jnp.concatenate([r[...].astype(BF16) for r in page_refs], axis=axis)

    sb_chunk(gathered(caches[0], 1), gathered(caches[1], 1), None, tri, True)
    diff_chunk(gathered(caches[2], 1), gathered(caches[3], 0), None, True)

    @pl.when(c == n_chunks - 1)
    def _():
        lam = _diff_lambda(lam_ref, lambda_init)
        o_sb = jnp.where(q_keep, acc_s_ref[...], 0.0)
        r_d = lax.broadcasted_iota(jnp.int32, acc_d_ref.shape, 0) % n_chunk_heads
        coef = jnp.where(r_d % 2 == 0, 1.0, -lam) / l_ref[...]
        o_d = acc_d_ref[...] * coef

        def per_query(x):
            return jnp.concatenate(
                [jnp.sum(x[q * n_chunk_heads:(q + 1) * n_chunk_heads], axis=0, keepdims=True)
                 for q in range(ts)], axis=0)

        o_sb = per_query(o_sb)
        normed = [_rms(per_query(jnp.where(r_d // 2 == h, o_d, 0.0)), sub_ref[...]) * (1.0 - lambda_init)
                  for h in range(n_dh)]
        o_ref[0] = jnp.concatenate([o_sb] + normed, axis=1).astype(o_ref.dtype)


def _sample_attention(page_table, caches, layer, qs_exp, qd_exp, new_kv, lam_vecs, subln, lambda_init):
    bs, n_pages = page_table.shape
    w, page = caches[0].shape[2], caches[0].shape[3]
    rows = qs_exp.shape[1]
    ts = rows // (w // 64)
    pps = PAGES_PER_STEP
    assert n_pages % pps == 0 and (pps * page) % MXU_DIM == 0 and MXU_DIM % page == 0
    n_chunks = n_pages // pps
    pt_flat = page_table.reshape(-1)

    def page_spec(i, shape):
        def imap(b, c, pt):
            return (layer, pt[b * n_pages + (n_chunks - 1 - c) * pps + i], 0, 0)
        return pl.BlockSpec((None, None) + shape, imap)

    in_specs, args = [], []
    for cache in caches:
        for i in range(pps):
            in_specs.append(page_spec(i, cache.shape[2:]))
            args.append(cache)
    per_b = lambda shape: pl.BlockSpec((1,) + shape, lambda b, c, pt: (b, 0, 0))
    in_specs += [per_b((rows, w)), per_b((rows, w))] + [per_b(a.shape[1:]) for a in new_kv]
    args += [qs_exp, qd_exp] + list(new_kv)
    in_specs += [pl.BlockSpec(lam_vecs.shape, lambda b, c, pt: (0, 0)),
                 pl.BlockSpec(subln.shape, lambda b, c, pt: (0, 0))]
    args += [lam_vecs, subln]
    grid_spec = pltpu.PrefetchScalarGridSpec(
        num_scalar_prefetch=1, grid=(bs, n_chunks), in_specs=in_specs,
        out_specs=pl.BlockSpec((1, ts, 2 * w), lambda b, c, pt: (b, 0, 0)),
        scratch_shapes=[pltpu.VMEM((rows, w), F32), pltpu.VMEM((rows, 1), F32),
                        pltpu.VMEM((rows, LANES), F32), pltpu.VMEM((rows, 1), F32),
                        pltpu.VMEM((rows, 1), F32)])
    return pl.pallas_call(
        functools.partial(_sample_attn_kernel, pps=pps, ts=ts, n_chunks=n_chunks, page=page,
                          lambda_init=lambda_init),
        grid_spec=grid_spec, out_shape=jax.ShapeDtypeStruct((bs, ts, 2 * w), BF16),
        compiler_params=_params(("parallel", "arbitrary")), name="sample_attention",
    )(pt_flat, *args)


def _mem_attn_kernel(q_ref, mk_ref, mv_ref, o_ref, *, n_heads):
    hd = q_ref.shape[2] // n_heads
    for h in range(n_heads):
        cols = slice(h * hd, (h + 1) * hd)
        s = _dot_nt(q_ref[0, :, cols], mk_ref[0, :, cols].astype(BF16))
        p = jnp.exp(s - jnp.max(s, axis=-1, keepdims=True))
        l = jnp.sum(p, axis=-1, keepdims=True)
        o = _dot(p.astype(BF16), mv_ref[0, :, cols].astype(BF16)) / l
        o_ref[0, :, cols] = o.astype(o_ref.dtype)


def _mem_attention(q, mk, mv, n_heads):
    b, t, w = q.shape
    tm = min(TOKEN_TILE, t)
    assert t % tm == 0
    m = mk.shape[1]
    return pl.pallas_call(
        functools.partial(_mem_attn_kernel, n_heads=n_heads), grid=(b, t // tm),
        in_specs=[pl.BlockSpec((1, tm, w), lambda bi, i: (bi, i, 0)),
                  pl.BlockSpec((1, m, w), lambda bi, i: (bi, 0, 0)),
                  pl.BlockSpec((1, m, w), lambda bi, i: (bi, 0, 0))],
        out_specs=pl.BlockSpec((1, tm, w), lambda bi, i: (bi, i, 0)),
        out_shape=jax.ShapeDtypeStruct((b, t, w), BF16),
        compiler_params=_params(("parallel", "arbitrary")), name="mem_attention",
    )(q, mk, mv)


def _ffn_kernel(h_ref, cin_ref, g_ref, wup_ref, wconv_ref, bconv_ref, wdown_ref, gfin_ref,
                y_ref, cout_ref, ubuf_g_ref, ubuf_v_ref, act_ref, *, shift, dff, final_norm):
    t = pl.program_id(1)
    tm = h_ref.shape[1]
    cr = cout_ref.shape[1]

    @pl.when(t == 0)
    def _():
        cout_ref[0] = cin_ref[0]

    h = h_ref[0]
    hn = _rms(h, g_ref[...]).astype(BF16)

    def conv(ubuf_ref, c0):
        cols = slice(c0, c0 + FF_CHUNK)
        u = _dot(hn, wup_ref[:, cols])
        ubuf_ref[0:cr, :] = cout_ref[0, :, cols]
        ubuf_ref[cr:cr + tm, :] = u
        out = bconv_ref[:, cols] + u * wconv_ref[CONV_W - 1:CONV_W, cols]
        for i in range(CONV_W - 1):
            back = (CONV_W - 1 - i) * shift
            out = out + ubuf_ref[cr - back:cr - back + tm, :] * wconv_ref[i:i + 1, cols]
        cout_ref[0, :, cols] = ubuf_ref[tm:tm + cr, :]
        return out

    for j in range(dff // FF_CHUNK):
        gate = conv(ubuf_g_ref, j * FF_CHUNK)
        val = conv(ubuf_v_ref, dff + j * FF_CHUNK)
        act = gate * (1.0 / (1.0 + jnp.exp(-gate))) * val
        act_ref[:, j * FF_CHUNK:(j + 1) * FF_CHUNK] = act.astype(BF16)

    out = h + _dot(act_ref[...], wdown_ref[...])
    y_ref[0] = _rms(out, gfin_ref[...]) if final_norm else out


def _conv_ffn(h, conv_in, g, w_up, w_conv, b_conv, w_down, g_final, final_norm, shift):
    ngrp, t, d = h.shape
    dff = w_down.shape[0]
    cr = conv_in.shape[1]
    tm = min(TOKEN_TILE, t)
    assert t % tm == 0 and dff % FF_CHUNK == 0 and cr >= (CONV_W - 1) * shift and tm >= cr
    return pl.pallas_call(
        functools.partial(_ffn_kernel, shift=shift, dff=dff, final_norm=final_norm),
        grid=(ngrp, t // tm),
        in_specs=[pl.BlockSpec((1, tm, d), lambda gi, i: (gi, i, 0)),
                  pl.BlockSpec((1, cr, 2 * dff), lambda gi, i: (gi, 0, 0),
                               pipeline_mode=pl.Buffered(1)),
                  _resident((1, d)), _resident(w_up.shape), _resident(w_conv.shape),
                  _resident((1, 2 * dff)), _resident(w_down.shape), _resident((1, d))],
        out_specs=[pl.BlockSpec((1, tm, d), lambda gi, i: (gi, i, 0)),
                   pl.BlockSpec((1, cr, 2 * dff), lambda gi, i: (gi, 0, 0))],
        out_shape=[jax.ShapeDtypeStruct((ngrp, t, d), F32),
                   jax.ShapeDtypeStruct((ngrp, cr, 2 * dff), F32)],
        scratch_shapes=[pltpu.VMEM((cr + tm, FF_CHUNK), F32), pltpu.VMEM((cr + tm, FF_CHUNK), F32),
                        pltpu.VMEM((tm, dff), BF16)],
        compiler_params=_params(("parallel", "arbitrary")), name="conv_ffn",
    )(h, conv_in, g.reshape(1, d), w_up, w_conv, b_conv.reshape(1, 2 * dff), w_down,
      g_final.reshape(1, d))


def kernel(x_prompt, x_sample, cache_sb_k, cache_sb_v, cache_diff_k, cache_diff_v, cache_mem_k, cache_mem_v, state_conv, page_table, mem_prompt, norm_mix, w_in, lambda_q1, lambda_k1, lambda_q2, lambda_k2, diff_subln, w_out, norm_mem_q, norm_mem_kv, w_mem_q, w_mem_k, w_mem_v, w_mem_o, norm_ffn, w_up, w_conv, b_conv, w_down, norm_final):
    bp, tp, d = x_prompt.shape
    bs, ts, _ = x_sample.shape
    depth, n_pool, page, sb_heads, sb_hd = cache_sb_k.shape
    diff_heads, diff_hd = cache_diff_k.shape[3], cache_diff_k.shape[5]
    mem_tokens, mem_heads, mem_hd = cache_mem_k.shape[2:]
    n_pages = page_table.shape[1]
    past_len = n_pages * page
    sb_w = sb_heads * sb_hd
    dqk_w = diff_heads * 2 * diff_hd
    dv_w = diff_heads * 2 * diff_hd
    mem_w = mem_heads * mem_hd
    dff = w_down.shape[1]
    assert sb_hd == 64 and diff_hd == 64 and sb_w == dqk_w == dv_w and ts <= SUBLANES
    w = sb_w
    n_chunk_heads = w // 64

    sb_scale = 1.0 / math.sqrt(sb_hd)
    diff_scale = 1.0 / math.sqrt(diff_hd)
    mem_scale = 1.0 / math.sqrt(mem_hd)
    assert math.log2(sb_scale).is_integer() and math.log2(mem_scale).is_integer()

    in_segs = [(0, w, sb_scale, False, False, True), (w, w, 1.0, False, True, True),
               (2 * w, w, 1.0, False, True, True), (3 * w, w, diff_scale, True, False, True),
               (4 * w, w, 1.0, True, True, True), (5 * w, w, 1.0, False, True, True)]
    rope_p = _rope_table(jnp.arange(tp, dtype=jnp.int32))
    rope_s = jnp.tile(_rope_table(past_len + jnp.arange(ts, dtype=jnp.int32)), (bs, 1))

    hp = x_prompt.reshape(bp * tp, d)
    hs = x_sample.reshape(bs * ts, d)
    outs = {k: [] for k in ("p_sbk", "p_sbv", "p_dk", "p_dv", "p_mk", "p_mv", "p_conv",
                            "s_sbk", "s_sbv", "s_dk", "s_dv", "s_conv")}
    conv_rows_p = SUBLANES
    for l in range(depth):
        lambda_init = 0.8 - 0.6 * math.exp(-0.3 * l)
        lam_vecs = jnp.stack([lambda_q1[l], lambda_k1[l], lambda_q2[l], lambda_k2[l]]).astype(F32)
        subln = diff_subln[l].reshape(1, -1)
        w_in_b = w_in[l].astype(BF16)
        w_out_b = w_out[l].astype(BF16)
        w_q_b = w_mem_q[l].astype(BF16)
        w_kv_b = jnp.concatenate([w_mem_k[l], w_mem_v[l]], axis=1).astype(BF16)
        w_o_b = w_mem_o[l].astype(BF16)
        w_up_b = w_up[l].astype(BF16)
        w_down_b = w_down[l].astype(BF16)
        q_seg = [(0, mem_w, mem_scale, False, False, True)]

        def mem_block(h2d, ngrp, mk, mv):
            (q,) = _norm_matmul(h2d, norm_mem_q[l], w_q_b, q_seg)
            t = h2d.shape[0] // ngrp
            o = _mem_attention(q.reshape(ngrp, t, mem_w), mk, mv, mem_heads)
            return _matmul_residual(o.reshape(ngrp * t, mem_w), w_o_b, h2d)

        qs, ksf, ksb, vsf, vsb, qd, kdf, kdb, vdf, vdb = _norm_matmul(
            hp, norm_mix[l], w_in_b, in_segs, rope_p)
        r3 = lambda a: a.reshape(bp, tp, w)
        o = _prompt_attention(r3(qs), r3(qd), r3(ksb), r3(vsb), r3(kdb), r3(vdb), lam_vecs, subln,
                              lambda_init)
        hp = _matmul_residual(o.reshape(bp * tp, 2 * w), w_out_b, hp)
        mkf, mvf = _norm_matmul(mem_prompt.reshape(bp * mem_tokens, d), norm_mem_kv[l], w_kv_b,
                                [(0, mem_w, 1.0, False, True, False),
                                 (mem_w, mem_w, 1.0, False, True, False)])
        mk3 = mkf.reshape(bp, mem_tokens, mem_w)
        mv3 = mvf.reshape(bp, mem_tokens, mem_w)
        hp = mem_block(hp, bp, mk3, mv3)
        conv0 = jnp.zeros((bp, conv_rows_p, 2 * dff), F32)
        last = l == depth - 1
        yp, conv_p = _conv_ffn(hp.reshape(bp, tp, d), conv0, norm_ffn[l], w_up_b, w_conv[l],
                               b_conv[l], w_down_b, norm_final, last, 1)
        outs["p_sbk"].append(ksf.reshape(bp, tp, sb_heads, sb_hd))
        outs["p_sbv"].append(vsf.reshape(bp, tp, sb_heads, sb_hd))
        outs["p_dk"].append(kdf.reshape(bp, tp, diff_heads, 2, diff_hd))
        outs["p_dv"].append(vdf.reshape(bp, tp, diff_heads, 2 * diff_hd))
        outs["p_mk"].append(mk3.reshape(bp, mem_tokens, mem_heads, mem_hd))
        outs["p_mv"].append(mv3.reshape(bp, mem_tokens, mem_heads, mem_hd))
        outs["p_conv"].append(conv_p[:, conv_rows_p - (CONV_W - 1):])

        qs, ksf, _, vsf, _, qd, kdf, _, vdf, _ = _norm_matmul(hs, norm_mix[l], w_in_b, in_segs, rope_s)
        expand = lambda q: jnp.repeat(q.reshape(bs, ts, w), n_chunk_heads, axis=1)
        pad_new = lambda a: jnp.pad(a.reshape(bs, ts, w), ((0, 0), (0, SUBLANES - ts), (0, 0)))
        caches = [jnp.moveaxis(cache_sb_k, 2, -1).reshape(depth, n_pool, w, page),
                  jnp.moveaxis(cache_sb_v, 2, -1).reshape(depth, n_pool, w, page),
                  jnp.moveaxis(cache_diff_k, 2, -1).reshape(depth, n_pool, w, page),
                  cache_diff_v.reshape(depth, n_pool, page * diff_heads, 2 * diff_hd)]
        new_kv = [pad_new(a) for a in (ksf, vsf, kdf, vdf)]
        new_kv[3] = new_kv[3].reshape(bs, SUBLANES * diff_heads, 2 * diff_hd)
        o = _sample_attention(page_table, caches, l, expand(qs), expand(qd), new_kv, lam_vecs, subln,
                              lambda_init)
        hs = _matmul_residual(o.reshape(bs * ts, 2 * w), w_out_b, hs)
        hs = mem_block(hs, bs, cache_mem_k[l].reshape(bs, mem_tokens, mem_w),
                       cache_mem_v[l].reshape(bs, mem_tokens, mem_w))
        hs_tm = hs.reshape(bs, ts, d).transpose(1, 0, 2).reshape(1, ts * bs, d)
        conv_in = state_conv[l].transpose(1, 0, 2).reshape(1, (CONV_W - 1) * bs, 2 * dff)
        ys, conv_s = _conv_ffn(hs_tm, conv_in, norm_ffn[l], w_up_b, w_conv[l], b_conv[l], w_down_b,
                               norm_final, last, bs)
        hs = ys.reshape(ts, bs, d).transpose(1, 0, 2).reshape(bs * ts, d)
        outs["s_sbk"].append(ksf.reshape(bs, ts, sb_heads, sb_hd))
        outs["s_sbv"].append(vsf.reshape(bs, ts, sb_heads, sb_hd))
        outs["s_dk"].append(kdf.reshape(bs, ts, diff_heads, 2, diff_hd))
        outs["s_dv"].append(vdf.reshape(bs, ts, diff_heads, 2 * diff_hd))
        outs["s_conv"].append(conv_s.reshape(CONV_W - 1, bs, 2 * dff).transpose(1, 0, 2))
        hp = yp.reshape(bp * tp, d)

    st = lambda xs: jnp.stack(xs, axis=0)
    return (hp.reshape(bp, tp, d), hs.reshape(bs, ts, d), st(outs["p_sbk"]), st(outs["p_sbv"]),
            st(outs["p_dk"]), st(outs["p_dv"]), st(outs["p_mk"]), st(outs["p_mv"]), st(outs["p_conv"]),
            st(outs["s_sbk"]), st(outs["s_sbv"]), st(outs["s_dk"]), st(outs["s_dv"]), st(outs["s_conv"]))
```

```python
import functools
import math

import jax
import jax.numpy as jnp
from jax import lax
from jax.experimental import pallas as pl
from jax.experimental.pallas import tpu as pltpu

F32 = jnp.float32
BF16 = jnp.bfloat16

EPS = 1e-6
ROPE_DIM = 16
ROPE_THETA = 500000.0
CONV_W = 3
NEG = -1e30

LANES = 128
SUBLANES = 8
MXU_DIM = 256
VMEM_LIMIT = 56 * 1024 * 1024

TOKEN_TILE = 512
ATTN_TILE = 256
PAGES_PER_STEP = 16
FF_CHUNK = 256


def _dot(a, b):
    return jnp.dot(a, b, preferred_element_type=F32)


def _dot_nt(a, b):
    return lax.dot_general(a, b, (((1,), (1,)), ((), ())), preferred_element_type=F32)


def _rms(x, g):
    return x * lax.rsqrt(jnp.mean(x * x, axis=-1, keepdims=True) + EPS) * g


def _resident(shape):
    nd = len(shape)
    return pl.BlockSpec(shape, lambda *_: (0,) * nd, pipeline_mode=pl.Buffered(1))


def _params(sem):
    return pltpu.CompilerParams(dimension_semantics=sem, vmem_limit_bytes=VMEM_LIMIT)


def _split_hi_lo(x):
    hi = x.astype(BF16)
    lo = (x - hi.astype(F32)).astype(BF16)
    return jnp.concatenate([hi, lo], axis=1)


def _suffix_sum_matrix(n):
    r = lax.broadcasted_iota(jnp.int32, (2 * n, n), 0)
    c = lax.broadcasted_iota(jnp.int32, (2 * n, n), 1)
    return jnp.where(jnp.where(r >= n, r - n, r) > c, 1.0, 0.0).astype(BF16)


def _neg_softplus(s):
    return -(jnp.maximum(s, 0.0) + jnp.log(1.0 + jnp.exp(-jnp.abs(s))))


def _rope_lanes(u, rope_ref):
    cos = rope_ref[:, 0:LANES]
    sin_lo = rope_ref[:, LANES:2 * LANES]
    sin_hi = rope_ref[:, 2 * LANES:3 * LANES]
    half = ROPE_DIM // 2
    pieces = []
    for j in range(u.shape[1] // LANES):
        uj = u[:, j * LANES:(j + 1) * LANES]
        pieces.append(uj * cos + pltpu.roll(uj, LANES - half, 1) * sin_lo
                      + pltpu.roll(uj, half, 1) * sin_hi)
    return jnp.concatenate(pieces, axis=1)


def _rope_rows(ut, rope_t_ref):
    half = ROPE_DIM // 2
    cos = rope_t_ref[0:half, :]
    sin = rope_t_ref[half:ROPE_DIM, :]
    pieces = []
    for c in range(ut.shape[0] // 64):
        x1 = ut[c * 64:c * 64 + half]
        x2 = ut[c * 64 + half:c * 64 + ROPE_DIM]
        pieces += [x1 * cos - x2 * sin, x2 * cos + x1 * sin, ut[c * 64 + ROPE_DIM:(c + 1) * 64]]
    return jnp.concatenate(pieces, axis=0)


def _prompt_in_proj_kernel(x_ref, g_ref, w_ref, wt_ref, rope_ref, rope_t_ref,
                           qs_ref, qd_ref, vd_ref, vdb_ref, kst_ref, ksb_ref, vst_ref, vsb_ref,
                           kdt_ref, kdb_ref, *, w, sb_scale, diff_scale):
    xn = _rms(x_ref[0], g_ref[...]).astype(BF16)
    qs_ref[0] = (_dot(xn, w_ref[:, 0:w]) * sb_scale).astype(BF16)
    qd_ref[0] = (_rope_lanes(_dot(xn, w_ref[:, 3 * w:4 * w]), rope_ref) * diff_scale).astype(BF16)
    vd = _dot(xn, w_ref[:, 5 * w:6 * w])
    vd_ref[0] = vd
    vdb_ref[0] = vd.astype(BF16)
    n_blk, _, tk = ksb_ref.shape[1:]
    for c0, rope, f_ref, b_ref in ((w, False, kst_ref, ksb_ref), (2 * w, False, vst_ref, vsb_ref),
                                   (4 * w, True, kdt_ref, kdb_ref)):
        ut = _dot_nt(wt_ref[c0:c0 + w, :], xn)
        if rope:
            ut = _rope_rows(ut, rope_t_ref)
        f_ref[0] = ut
        for j in range(n_blk):
            b_ref[0, j] = ut[:, j * tk:(j + 1) * tk].astype(BF16)


def _prompt_in_proj(x, g, w_bf16, wt_bf16, rope_tab, rope_t, w, sb_scale, diff_scale):
    b, t, d = x.shape
    tm = min(TOKEN_TILE, t)
    tk = ATTN_TILE
    assert t % tm == 0 and tm % tk == 0
    tok = lambda dt: (jax.ShapeDtypeStruct((b, t, w), dt), pl.BlockSpec((1, tm, w), lambda bi, i: (bi, i, 0)))
    feat = (jax.ShapeDtypeStruct((b, w, t), F32), pl.BlockSpec((1, w, tm), lambda bi, i: (bi, 0, i)))
    blk = (jax.ShapeDtypeStruct((b, t // tk, w, tk), BF16),
           pl.BlockSpec((1, tm // tk, w, tk), lambda bi, i: (bi, i, 0, 0)))
    outs = [tok(BF16), tok(BF16), tok(F32), tok(BF16), feat, blk, feat, blk, feat, blk]
    return pl.pallas_call(
        functools.partial(_prompt_in_proj_kernel, w=w, sb_scale=sb_scale, diff_scale=diff_scale),
        grid=(b, t // tm),
        in_specs=[pl.BlockSpec((1, tm, d), lambda bi, i: (bi, i, 0)), _resident((1, d)),
                  _resident(w_bf16.shape), _resident(wt_bf16.shape),
                  pl.BlockSpec((tm, 3 * LANES), lambda bi, i: (i, 0)),
                  pl.BlockSpec((ROPE_DIM, tm), lambda bi, i: (0, i))],
        out_specs=[o[1] for o in outs], out_shape=[o[0] for o in outs],
        compiler_params=_params(("parallel", "parallel")), name="prompt_in_proj",
    )(x, g.reshape(1, d), w_bf16, wt_bf16, rope_tab, rope_t)


def _norm_matmul_kernel(*refs, segs, has_rope):
    x_ref, g_ref, w_ref = refs[:3]
    rope_ref = refs[3] if has_rope else None
    outs = refs[4:] if has_rope else refs[3:]
    xn = _rms(x_ref[...], g_ref[...]).astype(BF16)
    oi = 0
    for c0, width, scale, rope, want_f32, want_bf16 in segs:
        u = _dot(xn, w_ref[:, c0:c0 + width])
        if rope:
            u = _rope_lanes(u, rope_ref)
        if want_f32:
            outs[oi][...] = u
            oi += 1
        if want_bf16:
            outs[oi][...] = (u * scale).astype(BF16)
            oi += 1


def _norm_matmul(x, g, w_bf16, segs, rope_tab=None):
    m, d = x.shape
    tm = min(TOKEN_TILE, m)
    assert m % tm == 0
    in_specs = [pl.BlockSpec((tm, d), lambda i: (i, 0)), _resident((1, d)), _resident(w_bf16.shape)]
    args = [x, g.reshape(1, d), w_bf16]
    if rope_tab is not None:
        nrt = rope_tab.shape[0] // tm
        assert rope_tab.shape[0] % tm == 0
        in_specs.append(pl.BlockSpec((tm, 3 * LANES), lambda i: (i % nrt, 0)))
        args.append(rope_tab)
    out_shape, out_specs = [], []
    for _, width, _, _, want_f32, want_bf16 in segs:
        for want, dt in ((want_f32, F32), (want_bf16, BF16)):
            if want:
                out_shape.append(jax.ShapeDtypeStruct((m, width), dt))
                out_specs.append(pl.BlockSpec((tm, width), lambda i: (i, 0)))
    return pl.pallas_call(
        functools.partial(_norm_matmul_kernel, segs=tuple(segs), has_rope=rope_tab is not None),
        grid=(m // tm,), in_specs=in_specs, out_specs=out_specs, out_shape=out_shape,
        compiler_params=_params(("parallel",)), name="norm_matmul",
    )(*args)


def _rope_table(pos):
    half = ROPE_DIM // 2
    inv_freq = ROPE_THETA ** (-jnp.arange(half, dtype=F32) * 2.0 / ROPE_DIM)
    ang = pos.astype(F32)[:, None] * inv_freq[None, :]
    cos, sin = jnp.cos(ang), jnp.sin(ang)
    n = pos.shape[0]
    ones = jnp.ones((n, 64 - ROPE_DIM), F32)
    zeros = jnp.zeros((n, 64 - ROPE_DIM), F32)
    zh = jnp.zeros((n, half), F32)
    c64 = jnp.concatenate([cos, cos, ones], axis=1)
    lo64 = jnp.concatenate([-sin, zh, zeros], axis=1)
    hi64 = jnp.concatenate([zh, sin, zeros], axis=1)
    rep = LANES // 64
    return jnp.concatenate([jnp.tile(c64, (1, rep)), jnp.tile(lo64, (1, rep)), jnp.tile(hi64, (1, rep))], axis=1)


def _rope_table_t(pos):
    half = ROPE_DIM // 2
    inv_freq = ROPE_THETA ** (-jnp.arange(half, dtype=F32) * 2.0 / ROPE_DIM)
    ang = pos.astype(F32)[:, None] * inv_freq[None, :]
    return jnp.concatenate([jnp.cos(ang).T, jnp.sin(ang).T], axis=0)


def _matmul_residual_kernel(a_ref, w_ref, r_ref, o_ref):
    o_ref[...] = r_ref[...] + _dot(a_ref[...].astype(BF16), w_ref[...])


def _matmul_residual(a, w_bf16, res):
    m, k = a.shape
    n = w_bf16.shape[1]
    tm = min(TOKEN_TILE, m)
    assert m % tm == 0
    return pl.pallas_call(
        _matmul_residual_kernel, grid=(m // tm,),
        in_specs=[pl.BlockSpec((tm, k), lambda i: (i, 0)), _resident(w_bf16.shape),
                  pl.BlockSpec((tm, n), lambda i: (i, 0))],
        out_specs=pl.BlockSpec((tm, n), lambda i: (i, 0)),
        out_shape=jax.ShapeDtypeStruct((m, n), F32),
        compiler_params=_params(("parallel",)), name="matmul_residual",
    )(a, w_bf16, res)


def _diff_lambda(lam_ref, lambda_init):
    a = jnp.sum(lam_ref[0:1, :] * lam_ref[1:2, :], axis=-1, keepdims=True)
    b = jnp.sum(lam_ref[2:3, :] * lam_ref[3:4, :], axis=-1, keepdims=True)
    return jnp.exp(a) - jnp.exp(b) + lambda_init


def _sb_weights(s, run, tri, mask):
    n = s.shape[1]
    blk = tri.shape[1]
    nb = n // blk
    log1m = _neg_softplus(s)
    log_sig = s + log1m
    if mask is not None:
        log1m = jnp.where(mask, log1m, 0.0)
    stacked = jnp.concatenate([log1m[:, b * blk:(b + 1) * blk] for b in range(nb)], axis=0)
    loc = _dot(_split_hi_lo(stacked), tri)
    rows = s.shape[0]
    after = []
    for b in reversed(range(nb)):
        after.append(loc[b * rows:(b + 1) * rows] + _lane_tile(run, blk))
        run = run + jnp.sum(log1m[:, b * blk:(b + 1) * blk], axis=-1, keepdims=True)
    after = jnp.concatenate(after[::-1], axis=1) if nb > 1 else after[0]
    a = jnp.exp(log_sig + after)
    if mask is not None:
        a = jnp.where(mask, a, 0.0)
    return a, run


def _lane_tile(x, n):
    reps = n // x.shape[1]
    return x if reps == 1 else jnp.concatenate([x] * reps, axis=1)


def _softmax_step(s, m_prev, l_prev):
    m_new = jnp.maximum(m_prev, jnp.max(s, axis=-1, keepdims=True))
    alpha = jnp.exp(m_prev - m_new)
    p = jnp.exp(s - _lane_tile(m_new, s.shape[1]))
    l_new = alpha * l_prev + jnp.sum(p, axis=-1, keepdims=True)
    return p, alpha, m_new, l_new


def _prompt_attn_kernel(qs_ref, qd_ref, ks_ref, vs_ref, kd_ref, vd_ref, lam_ref, sub_ref, o_ref,
                        q2_ref, acc_ref, st0_ref, st1_ref, *, tq, lambda_init, n_pairs):
    i = pl.program_id(1)
    rows = 2 * tq
    lane = lax.broadcasted_iota(jnp.int32, (rows, LANES), 1)
    row = lax.broadcasted_iota(jnp.int32, (rows, LANES), 0)
    keep = (lane < 64) == (row < tq)
    qrow = lax.broadcasted_iota(jnp.int32, (rows, tq), 0)
    qidx = jnp.where(qrow < tq, qrow, qrow - tq)
    kidx = lax.broadcasted_iota(jnp.int32, (rows, tq), 1)
    mask_strict = kidx < qidx
    mask_incl = kidx <= qidx
    tri = _suffix_sum_matrix(tq)
    lam = _diff_lambda(lam_ref, lambda_init)

    for p in range(n_pairs):
        for c, q_ref in ((p, qs_ref), (n_pairs + p, qd_ref)):
            qp = q_ref[0, :, p * LANES:(p + 1) * LANES]
            q2 = jnp.concatenate([qp, qp], axis=0).astype(F32)
            q2_ref[c] = jnp.where(keep, q2, 0.0).astype(BF16)
    acc_ref[...] = jnp.zeros_like(acc_ref)
    st0_ref[0:n_pairs] = jnp.zeros((n_pairs, rows, LANES), F32)
    st0_ref[n_pairs:2 * n_pairs] = jnp.full((n_pairs, rows, LANES), NEG, F32)
    st1_ref[...] = jnp.zeros_like(st1_ref)

    def key_block(j, mask_sb, mask_diff):
        krows = pl.ds(pl.multiple_of(j * tq, tq), tq)
        for p in range(n_pairs):
            cols = slice(p * LANES, (p + 1) * LANES)
            a, run = _sb_weights(_dot(q2_ref[p], ks_ref[0, j, cols, :]), st0_ref[p], tri, mask_sb)
            acc_ref[p] += _dot_nt(a.astype(BF16), vs_ref[0, j, cols, :])
            st0_ref[p] = run
            c = n_pairs + p
            s = _dot(q2_ref[c], kd_ref[0, j, cols, :])
            if mask_diff is not None:
                s = jnp.where(mask_diff, s, NEG)
            pr, alpha, m_new, l_new = _softmax_step(s, st0_ref[c], st1_ref[p])
            acc_ref[c] = alpha * acc_ref[c] + _dot(pr.astype(BF16), vd_ref[0, krows, cols])
            st0_ref[c] = m_new
            st1_ref[p] = l_new

    key_block(i, mask_strict, mask_incl)

    def below_diagonal(t, carry):
        key_block(i - 1 - t, None, None)
        return carry

    lax.fori_loop(0, i, below_diagonal, 0)

    first_chunk = lax.broadcasted_iota(jnp.int32, (tq, LANES), 1) < 64
    for p in range(n_pairs):
        acc = acc_ref[p]
        o_ref[0, :, p * LANES:(p + 1) * LANES] = jnp.where(first_chunk, acc[:tq], acc[tq:]).astype(o_ref.dtype)
        w = acc_ref[n_pairs + p] / st1_ref[p]
        od = _rms(w[:tq] - lam * w[tq:], sub_ref[...]) * (1.0 - lambda_init)
        o_ref[0, :, (n_pairs + p) * LANES:(n_pairs + p + 1) * LANES] = od.astype(o_ref.dtype)


def _prompt_attention(qs, qd, ks, vs, kd, vd, lam_vecs, subln, lambda_init):
    b, t, w = qs.shape
    tq = ATTN_TILE
    n_pairs = w // LANES
    assert t % tq == 0 and w % LANES == 0 and ks.shape == (b, t // tq, w, tq)
    qspec = pl.BlockSpec((1, tq, w), lambda bi, i: (bi, i, 0))
    kvspec = pl.BlockSpec((1, t, w), lambda bi, i: (bi, 0, 0))
    fspec = pl.BlockSpec((1, t // tq, w, tq), lambda bi, i: (bi, 0, 0, 0))
    return pl.pallas_call(
        functools.partial(_prompt_attn_kernel, tq=tq, lambda_init=lambda_init, n_pairs=n_pairs),
        grid=(b, t // tq),
        in_specs=[qspec, qspec, fspec, fspec, fspec, kvspec,
                  _resident(lam_vecs.shape), _resident(subln.shape)],
        out_specs=pl.BlockSpec((1, tq, 2 * w), lambda bi, i: (bi, i, 0)),
        out_shape=jax.ShapeDtypeStruct((b, t, 2 * w), BF16),
        scratch_shapes=[pltpu.VMEM((2 * n_pairs, 2 * tq, LANES), BF16),
                        pltpu.VMEM((2 * n_pairs, 2 * tq, LANES), F32),
                        pltpu.VMEM((2 * n_pairs, 2 * tq, LANES), F32),
                        pltpu.VMEM((n_pairs, 2 * tq, LANES), F32)],
        compiler_params=_params(("parallel", "arbitrary")), name="prompt_attention",
    )(qs, qd, ks, vs, kd, vd, lam_vecs, subln)


def _sample_attn_kernel(pt_ref, *refs, pps, ts, n_chunks, page, lambda_init):
    del pt_ref
    caches = [refs[c * pps:(c + 1) * pps] for c in range(4)]
    (qs_ref, qd_ref, nks_ref, nvs_ref, nkd_ref, nvd_ref, lam_ref, sub_ref, o_ref,
     acc_s_ref, run_ref, acc_d_ref, m_ref, l_ref) = refs[4 * pps:]
    c = pl.program_id(1)
    rows, w = acc_s_ref.shape
    n_chunk_heads = rows // ts
    tri = _suffix_sum_matrix(MXU_DIM)

    r_w = lax.broadcasted_iota(jnp.int32, (rows, w), 0)
    c_w = lax.broadcasted_iota(jnp.int32, (rows, w), 1)
    chunk_of_row = r_w % n_chunk_heads
    q_keep = (c_w // 64) == chunk_of_row

    def block_diag_q(q_ref):
        return jnp.where(q_keep, q_ref[0].astype(F32), 0.0).astype(BF16)

    q_sb = block_diag_q(qs_ref)
    q_d = block_diag_q(qd_ref)

    def sb_chunk(k, v, mask, tri_blk, feature_major):
        s = _dot(q_sb, k) if feature_major else _dot_nt(q_sb, k)
        a, run = _sb_weights(s, run_ref[...], tri_blk, mask)
        a = a.astype(BF16)
        acc_s_ref[...] += _dot_nt(a, v) if feature_major else _dot(a, v)
        run_ref[...] = run

    n_dh = w // LANES
    t_i = lax.broadcasted_iota(jnp.int32, (page, n_dh * page), 0)
    j_i = lax.broadcasted_iota(jnp.int32, (page, n_dh * page), 1)
    spread = jnp.where(j_i // n_dh == t_i, 1.0, 0.0).astype(BF16)
    own_head = (lax.broadcasted_iota(jnp.int32, (rows, n_dh * page), 1) % n_dh
                == (lax.broadcasted_iota(jnp.int32, (rows, n_dh * page), 0) % n_chunk_heads) // 2)

    def diff_chunk(k, v_rows, mask, feature_major):
        s = _dot(q_d, k) if feature_major else _dot_nt(q_d, k)
        if mask is not None:
            s = jnp.where(mask, s, NEG)
        pr, alpha, m_new, l_new = _softmax_step(s, m_ref[...], l_ref[...])
        pr = pr.astype(BF16)
        n_pg = pr.shape[1] // page
        stacked = jnp.concatenate([pr[:, i * page:(i + 1) * page] for i in range(n_pg)], axis=0)
        wide = _dot(stacked, spread)
        wide = jnp.concatenate(
            [jnp.where(own_head, wide[i * rows:(i + 1) * rows], 0.0) for i in range(n_pg)], axis=1)
        acc_d_ref[...] = alpha * acc_d_ref[...] + _dot(wide.astype(BF16), v_rows)
        m_ref[...] = m_new
        l_ref[...] = l_new

    @pl.when(c == 0)
    def _():
        acc_s_ref[...] = jnp.zeros_like(acc_s_ref)
        run_ref[...] = jnp.zeros_like(run_ref)
        acc_d_ref[...] = jnp.zeros_like(acc_d_ref)
        m_ref[...] = jnp.full_like(m_ref, NEG)
        l_ref[...] = jnp.zeros_like(l_ref)
        def padded(ref, n_rows):
            x = ref[0]
            pad = jnp.zeros((n_rows - x.shape[0], x.shape[1]), x.dtype)
            return jnp.concatenate([x, pad], axis=0).astype(BF16)
        key = lax.broadcasted_iota(jnp.int32, (rows, page), 1)
        qpos = lax.broadcasted_iota(jnp.int32, (rows, page), 0) // n_chunk_heads
        sb_chunk(padded(nks_ref, page), padded(nvs_ref, page), key < qpos, _suffix_sum_matrix(page), False)
        diff_chunk(padded(nkd_ref, page), padded(nvd_ref, n_dh * page), key <= qpos, False)

    def gathered(page_refs, axis):
        return jnp.concatenate([r[...].astype(BF16) for r in page_refs], axis=axis)

    sb_chunk(gathered(caches[0], 1), gathered(caches[1], 1), None, tri, True)
    diff_chunk(gathered(caches[2], 1), gathered(caches[3], 0), None, True)

    @pl.when(c == n_chunks - 1)
    def _():
        lam = _diff_lambda(lam_ref, lambda_init)
        o_sb = jnp.where(q_keep, acc_s_ref[...], 0.0)
        r_d = lax.broadcasted_iota(jnp.int32, acc_d_ref.shape, 0) % n_chunk_heads
        coef = jnp.where(r_d % 2 == 0, 1.0, -lam) / l_ref[...]
        o_d = acc_d_ref[...] * coef

        def per_query(x):
            return jnp.concatenate(
                [jnp.sum(x[q * n_chunk_heads:(q + 1) * n_chunk_heads], axis=0, keepdims=True)
                 for q in range(ts)], axis=0)

        o_sb = per_query(o_sb)
        normed = [_rms(per_query(jnp.where(r_d // 2 == h, o_d, 0.0)), sub_ref[...]) * (1.0 - lambda_init)
                  for h in range(n_dh)]
        o_ref[0] = jnp.concatenate([o_sb] + normed, axis=1).astype(o_ref.dtype)


def _sample_attention(page_table, caches, layer, qs_exp, qd_exp, new_kv, lam_vecs, subln, lambda_init):
    bs, n_pages = page_table.shape
    w, page = caches[0].shape[2], caches[0].shape[3]
    rows = qs_exp.shape[1]
    ts = rows // (w // 64)
    pps = PAGES_PER_STEP
    assert n_pages % pps == 0 and (pps * page) % MXU_DIM == 0 and MXU_DIM % page == 0
    n_chunks = n_pages // pps
    pt_flat = page_table.reshape(-1)

    def page_spec(i, shape):
        def imap(b, c, pt):
            return (layer, pt[b * n_pages + (n_chunks - 1 - c) * pps + i], 0, 0)
        return pl.BlockSpec((None, None) + shape, imap)

    in_specs, args = [], []
    for cache in caches:
        for i in range(pps):
            in_specs.append(page_spec(i, cache.shape[2:]))
            args.append(cache)
    per_b = lambda shape: pl.BlockSpec((1,) + shape, lambda b, c, pt: (b, 0, 0))
    in_specs += [per_b((rows, w)), per_b((rows, w))] + [per_b(a.shape[1:]) for a in new_kv]
    args += [qs_exp, qd_exp] + list(new_kv)
    in_specs += [pl.BlockSpec(lam_vecs.shape, lambda b, c, pt: (0, 0)),
                 pl.BlockSpec(subln.shape, lambda b, c, pt: (0, 0))]
    args += [lam_vecs, subln]
    grid_spec = pltpu.PrefetchScalarGridSpec(
        num_scalar_prefetch=1, grid=(bs, n_chunks), in_specs=in_specs,
        out_specs=pl.BlockSpec((1, ts, 2 * w), lambda b, c, pt: (b, 0, 0)),
        scratch_shapes=[pltpu.VMEM((rows, w), F32), pltpu.VMEM((rows, LANES), F32),
                        pltpu.VMEM((rows, LANES), F32), pltpu.VMEM((rows, LANES), F32),
                        pltpu.VMEM((rows, LANES), F32)])
    return pl.pallas_call(
        functools.partial(_sample_attn_kernel, pps=pps, ts=ts, n_chunks=n_chunks, page=page,
                          lambda_init=lambda_init),
        grid_spec=grid_spec, out_shape=jax.ShapeDtypeStruct((bs, ts, 2 * w), BF16),
        compiler_params=_params(("parallel", "arbitrary")), name="sample_attention",
    )(pt_flat, *args)


def _mem_attn_kernel(q_ref, mk_ref, mv_ref, o_ref, *, n_heads):
    hd = q_ref.shape[2] // n_heads
    for h in range(n_heads):
        cols = slice(h * hd, (h + 1) * hd)
        if len(mk_ref.shape) == 4:
            mk, mv = mk_ref[0, :, h, :], mv_ref[0, :, h, :]
        else:
            mk, mv = mk_ref[0, :, cols], mv_ref[0, :, cols]
        s = _dot_nt(q_ref[0, :, cols], mk.astype(BF16))
        p = jnp.exp(s - jnp.max(s, axis=-1, keepdims=True))
        l = jnp.sum(p, axis=-1, keepdims=True)
        o = _dot(p.astype(BF16), mv.astype(BF16)) / l
        o_ref[0, :, cols] = o.astype(o_ref.dtype)


def _mem_attention(q, mk, mv, n_heads):
    b, t, w = q.shape
    tm = min(TOKEN_TILE, t)
    assert t % tm == 0
    kv_block = (1,) + mk.shape[1:]
    kv_map = (lambda bi, i: (bi, 0, 0)) if mk.ndim == 3 else (lambda bi, i: (bi, 0, 0, 0))
    return pl.pallas_call(
        functools.partial(_mem_attn_kernel, n_heads=n_heads), grid=(b, t // tm),
        in_specs=[pl.BlockSpec((1, tm, w), lambda bi, i: (bi, i, 0)),
                  pl.BlockSpec(kv_block, kv_map), pl.BlockSpec(kv_block, kv_map)],
        out_specs=pl.BlockSpec((1, tm, w), lambda bi, i: (bi, i, 0)),
        out_shape=jax.ShapeDtypeStruct((b, t, w), BF16),
        compiler_params=_params(("parallel", "arbitrary")), name="mem_attention",
    )(q, mk, mv)


def _ffn_kernel(h_ref, cin_ref, g_ref, wup_ref, wconv_ref, bconv_ref, wdown_ref, gfin_ref,
                y_ref, cout_ref, ubuf_g_ref, ubuf_v_ref, act_ref, *, shift, dff, final_norm):
    t = pl.program_id(1)
    tm = h_ref.shape[1]
    cr = cout_ref.shape[1]

    @pl.when(t == 0)
    def _():
        cout_ref[0] = cin_ref[0]

    h = h_ref[0]
    hn = _rms(h, g_ref[...]).astype(BF16)

    def conv(ubuf_ref, c0):
        cols = slice(c0, c0 + FF_CHUNK)
        u = _dot(hn, wup_ref[:, cols])
        ubuf_ref[0:cr, :] = cout_ref[0, :, cols]
        ubuf_ref[cr:cr + tm, :] = u
        out = bconv_ref[:, cols] + u * wconv_ref[CONV_W - 1:CONV_W, cols]
        for i in range(CONV_W - 1):
            back = (CONV_W - 1 - i) * shift
            out = out + ubuf_ref[cr - back:cr - back + tm, :] * wconv_ref[i:i + 1, cols]
        cout_ref[0, :, cols] = ubuf_ref[tm:tm + cr, :]
        return out

    for j in range(dff // FF_CHUNK):
        gate = conv(ubuf_g_ref, j * FF_CHUNK)
        val = conv(ubuf_v_ref, dff + j * FF_CHUNK)
        act = gate * (1.0 / (1.0 + jnp.exp(-gate))) * val
        act_ref[:, j * FF_CHUNK:(j + 1) * FF_CHUNK] = act.astype(BF16)

    out = h + _dot(act_ref[...], wdown_ref[...])
    y_ref[0] = _rms(out, gfin_ref[...]) if final_norm else out


def _conv_ffn(h, conv_in, g, w_up, w_conv, b_conv, w_down, g_final, final_norm, shift):
    ngrp, t, d = h.shape
    dff = w_down.shape[0]
    cr = conv_in.shape[1]
    tm = min(TOKEN_TILE, t)
    assert t % tm == 0 and dff % FF_CHUNK == 0 and cr >= (CONV_W - 1) * shift and tm >= cr
    return pl.pallas_call(
        functools.partial(_ffn_kernel, shift=shift, dff=dff, final_norm=final_norm),
        grid=(ngrp, t // tm),
        in_specs=[pl.BlockSpec((1, tm, d), lambda gi, i: (gi, i, 0)),
                  pl.BlockSpec((1, cr, 2 * dff), lambda gi, i: (gi, 0, 0),
                               pipeline_mode=pl.Buffered(1)),
                  _resident((1, d)), _resident(w_up.shape), _resident(w_conv.shape),
                  _resident((1, 2 * dff)), _resident(w_down.shape), _resident((1, d))],
        out_specs=[pl.BlockSpec((1, tm, d), lambda gi, i: (gi, i, 0)),
                   pl.BlockSpec((1, cr, 2 * dff), lambda gi, i: (gi, 0, 0))],
        out_shape=[jax.ShapeDtypeStruct((ngrp, t, d), F32),
                   jax.ShapeDtypeStruct((ngrp, cr, 2 * dff), F32)],
        scratch_shapes=[pltpu.VMEM((cr + tm, FF_CHUNK), F32), pltpu.VMEM((cr + tm, FF_CHUNK), F32),
                        pltpu.VMEM((tm, dff), BF16)],
        compiler_params=_params(("parallel", "arbitrary")), name="conv_ffn",
    )(h, conv_in, g.reshape(1, d), w_up, w_conv, b_conv.reshape(1, 2 * dff), w_down,
      g_final.reshape(1, d))


def kernel(x_prompt, x_sample, cache_sb_k, cache_sb_v, cache_diff_k, cache_diff_v, cache_mem_k, cache_mem_v, state_conv, page_table, mem_prompt, norm_mix, w_in, lambda_q1, lambda_k1, lambda_q2, lambda_k2, diff_subln, w_out, norm_mem_q, norm_mem_kv, w_mem_q, w_mem_k, w_mem_v, w_mem_o, norm_ffn, w_up, w_conv, b_conv, w_down, norm_final):
    bp, tp, d = x_prompt.shape
    bs, ts, _ = x_sample.shape
    depth, n_pool, page, sb_heads, sb_hd = cache_sb_k.shape
    diff_heads, diff_hd = cache_diff_k.shape[3], cache_diff_k.shape[5]
    mem_tokens, mem_heads, mem_hd = cache_mem_k.shape[2:]
    n_pages = page_table.shape[1]
    past_len = n_pages * page
    sb_w = sb_heads * sb_hd
    dqk_w = diff_heads * 2 * diff_hd
    dv_w = diff_heads * 2 * diff_hd
    mem_w = mem_heads * mem_hd
    dff = w_down.shape[1]
    assert sb_hd == 64 and diff_hd == 64 and sb_w == dqk_w == dv_w and ts <= SUBLANES
    w = sb_w
    n_chunk_heads = w // 64

    sb_scale = 1.0 / math.sqrt(sb_hd)
    diff_scale = 1.0 / math.sqrt(diff_hd)
    mem_scale = 1.0 / math.sqrt(mem_hd)
    assert math.log2(sb_scale).is_integer() and math.log2(mem_scale).is_integer()

    in_segs = [(0, w, sb_scale, False, False, True), (w, w, 1.0, False, True, True),
               (2 * w, w, 1.0, False, True, True), (3 * w, w, diff_scale, True, False, True),
               (4 * w, w, 1.0, True, True, True), (5 * w, w, 1.0, False, True, True)]
    rope_p = _rope_table(jnp.arange(tp, dtype=jnp.int32))
    rope_pt = _rope_table_t(jnp.arange(tp, dtype=jnp.int32))
    rope_s = jnp.tile(_rope_table(past_len + jnp.arange(ts, dtype=jnp.int32)), (bs, 1))

    hp = x_prompt.reshape(bp * tp, d)
    hs = x_sample.reshape(bs * ts, d)
    outs = {k: [] for k in ("p_sbk", "p_sbv", "p_dk", "p_dv", "p_mk", "p_mv", "p_conv",
                            "s_sbk", "s_sbv", "s_dk", "s_dv", "s_conv")}
    conv_rows_p = SUBLANES
    for l in range(depth):
        lambda_init = 0.8 - 0.6 * math.exp(-0.3 * l)
        lam_vecs = jnp.stack([lambda_q1[l], lambda_k1[l], lambda_q2[l], lambda_k2[l]]).astype(F32)
        subln = diff_subln[l].reshape(1, -1)
        w_in_b = w_in[l].astype(BF16)
        w_out_b = w_out[l].astype(BF16)
        w_q_b = w_mem_q[l].astype(BF16)
        w_kv_b = jnp.concatenate([w_mem_k[l], w_mem_v[l]], axis=1).astype(BF16)
        w_o_b = w_mem_o[l].astype(BF16)
        w_up_b = w_up[l].astype(BF16)
        w_down_b = w_down[l].astype(BF16)
        q_seg = [(0, mem_w, mem_scale, False, False, True)]

        def mem_block(h2d, ngrp, mk, mv):
            (q,) = _norm_matmul(h2d, norm_mem_q[l], w_q_b, q_seg)
            t = h2d.shape[0] // ngrp
            o = _mem_attention(q.reshape(ngrp, t, mem_w), mk, mv, mem_heads)
            return _matmul_residual(o.reshape(ngrp * t, mem_w), w_o_b, h2d)

        qs, qd, vdf, vdb, kst, ksb, vst, vsb, kdt, kdb = _prompt_in_proj(
            hp.reshape(bp, tp, d), norm_mix[l], w_in_b, w_in[l].T.astype(BF16), rope_p, rope_pt, w,
            sb_scale, diff_scale)
        o = _prompt_attention(qs, qd, ksb, vsb, kdb, vdb, lam_vecs, subln, lambda_init)
        hp = _matmul_residual(o.reshape(bp * tp, 2 * w), w_out_b, hp)
        mkf, mvf = _norm_matmul(mem_prompt.reshape(bp * mem_tokens, d), norm_mem_kv[l], w_kv_b,
                                [(0, mem_w, 1.0, False, True, False),
                                 (mem_w, mem_w, 1.0, False, True, False)])
        mk3 = mkf.reshape(bp, mem_tokens, mem_w)
        mv3 = mvf.reshape(bp, mem_tokens, mem_w)
        hp = mem_block(hp, bp, mk3, mv3)
        conv0 = jnp.zeros((bp, conv_rows_p, 2 * dff), F32)
        last = l == depth - 1
        yp, conv_p = _conv_ffn(hp.reshape(bp, tp, d), conv0, norm_ffn[l], w_up_b, w_conv[l],
                               b_conv[l], w_down_b, norm_final, last, 1)
        outs["p_sbk"].append(jnp.moveaxis(kst.reshape(bp, sb_heads, sb_hd, tp), -1, 1))
        outs["p_sbv"].append(jnp.moveaxis(vst.reshape(bp, sb_heads, sb_hd, tp), -1, 1))
        outs["p_dk"].append(jnp.moveaxis(kdt.reshape(bp, diff_heads, 2, diff_hd, tp), -1, 1))
        outs["p_dv"].append(vdf.reshape(bp, tp, diff_heads, 2 * diff_hd))
        outs["p_mk"].append(mk3.reshape(bp, mem_tokens, mem_heads, mem_hd))
        outs["p_mv"].append(mv3.reshape(bp, mem_tokens, mem_heads, mem_hd))
        outs["p_conv"].append(conv_p[:, conv_rows_p - (CONV_W - 1):])

        qs, ksf, _, vsf, _, qd, kdf, _, vdf, _ = _norm_matmul(hs, norm_mix[l], w_in_b, in_segs, rope_s)
        expand = lambda q: jnp.repeat(q.reshape(bs, ts, w), n_chunk_heads, axis=1)
        pad_new = lambda a: jnp.pad(a.reshape(bs, ts, w), ((0, 0), (0, SUBLANES - ts), (0, 0)))
        caches = [jnp.moveaxis(cache_sb_k, 2, -1).reshape(depth, n_pool, w, page),
                  jnp.moveaxis(cache_sb_v, 2, -1).reshape(depth, n_pool, w, page),
                  jnp.moveaxis(cache_diff_k, 2, -1).reshape(depth, n_pool, w, page),
                  cache_diff_v.reshape(depth, n_pool, page * diff_heads, 2 * diff_hd)]
        new_kv = [pad_new(a) for a in (ksf, vsf, kdf, vdf)]
        new_kv[3] = new_kv[3].reshape(bs, SUBLANES * diff_heads, 2 * diff_hd)
        o = _sample_attention(page_table, caches, l, expand(qs), expand(qd), new_kv, lam_vecs, subln,
                              lambda_init)
        hs = _matmul_residual(o.reshape(bs * ts, 2 * w), w_out_b, hs)
        hs = mem_block(hs, bs, cache_mem_k[l], cache_mem_v[l])
        hs_tm = hs.reshape(bs, ts, d).transpose(1, 0, 2).reshape(1, ts * bs, d)
        conv_in = state_conv[l].transpose(1, 0, 2).reshape(1, (CONV_W - 1) * bs, 2 * dff)
        ys, conv_s = _conv_ffn(hs_tm, conv_in, norm_ffn[l], w_up_b, w_conv[l], b_conv[l], w_down_b,
                               norm_final, last, bs)
        hs = ys.reshape(ts, bs, d).transpose(1, 0, 2).reshape(bs * ts, d)
        outs["s_sbk"].append(ksf.reshape(bs, ts, sb_heads, sb_hd))
        outs["s_sbv"].append(vsf.reshape(bs, ts, sb_heads, sb_hd))
        outs["s_dk"].append(kdf.reshape(bs, ts, diff_heads, 2, diff_hd))
        outs["s_dv"].append(vdf.reshape(bs, ts, diff_heads, 2 * diff_hd))
        outs["s_conv"].append(conv_s.reshape(CONV_W - 1, bs, 2 * dff).transpose(1, 0, 2))
        hp = yp.reshape(bp * tp, d)

    st = lambda xs: jnp.stack(xs, axis=0)
    return (hp.reshape(bp, tp, d), hs.reshape(bs, ts, d), st(outs["p_sbk"]), st(outs["p_sbv"]),
            st(outs["p_dk"]), st(outs["p_dv"]), st(outs["p_mk"]), st(outs["p_mv"]), st(outs["p_conv"]),
            st(outs["s_sbk"]), st(outs["s_sbv"]), st(outs["s_dk"]), st(outs["s_dv"]), st(outs["s_conv"]))
```

```python
import functools
import math

import jax
import jax.numpy as jnp
from jax import lax
from jax.experimental import pallas as pl
from jax.experimental.pallas import tpu as pltpu

F32 = jnp.float32
BF16 = jnp.bfloat16

EPS = 1e-6
ROPE_DIM = 16
ROPE_THETA = 500000.0
CONV_W = 3
NEG = -1e30

LANES = 128
SUBLANES = 8
MXU_DIM = 256
VMEM_LIMIT = 56 * 1024 * 1024

TOKEN_TILE = 512
ATTN_TILE = 256
PAGES_PER_STEP = 16
FF_CHUNK = 256


def _dot(a, b):
    return jnp.dot(a, b, preferred_element_type=F32)


def _dot_nt(a, b):
    return lax.dot_general(a, b, (((1,), (1,)), ((), ())), preferred_element_type=F32)


def _rms(x, g):
    return x * lax.rsqrt(jnp.mean(x * x, axis=-1, keepdims=True) + EPS) * g


def _resident(shape):
    nd = len(shape)
    return pl.BlockSpec(shape, lambda *_: (0,) * nd, pipeline_mode=pl.Buffered(1))


def _params(sem):
    return pltpu.CompilerParams(dimension_semantics=sem, vmem_limit_bytes=VMEM_LIMIT)


def _split_hi_lo(x):
    hi = x.astype(BF16)
    lo = (x - hi.astype(F32)).astype(BF16)
    return jnp.concatenate([hi, lo], axis=1)


def _suffix_sum_matrix(n):
    r = lax.broadcasted_iota(jnp.int32, (2 * n, n), 0)
    c = lax.broadcasted_iota(jnp.int32, (2 * n, n), 1)
    return jnp.where(jnp.where(r >= n, r - n, r) > c, 1.0, 0.0).astype(BF16)


def _log2_one_minus_sigmoid(s2):
    return -(jnp.maximum(s2, 0.0) + jnp.log2(1.0 + jnp.exp2(-jnp.abs(s2))))


def _rope_lanes(u, rope_ref):
    cos = rope_ref[:, 0:LANES]
    sin_lo = rope_ref[:, LANES:2 * LANES]
    sin_hi = rope_ref[:, 2 * LANES:3 * LANES]
    half = ROPE_DIM // 2
    pieces = []
    for j in range(u.shape[1] // LANES):
        uj = u[:, j * LANES:(j + 1) * LANES]
        pieces.append(uj * cos + pltpu.roll(uj, LANES - half, 1) * sin_lo
                      + pltpu.roll(uj, half, 1) * sin_hi)
    return jnp.concatenate(pieces, axis=1)


def _rope_rows(ut, rope_t_ref):
    half = ROPE_DIM // 2
    cos = rope_t_ref[0:half, :]
    sin = rope_t_ref[half:ROPE_DIM, :]
    pieces = []
    for c in range(ut.shape[0] // 64):
        x1 = ut[c * 64:c * 64 + half]
        x2 = ut[c * 64 + half:c * 64 + ROPE_DIM]
        pieces += [x1 * cos - x2 * sin, x2 * cos + x1 * sin, ut[c * 64 + ROPE_DIM:(c + 1) * 64]]
    return jnp.concatenate(pieces, axis=0)


def _prompt_in_proj_kernel(x_ref, g_ref, w_ref, wt_ref, rope_ref, rope_t_ref,
                           qs_ref, qd_ref, vd_ref, vdb_ref, kst_ref, ksb_ref, vst_ref, vsb_ref,
                           kdt_ref, kdb_ref, *, w, sb_scale, diff_scale):
    xn = _rms(x_ref[0], g_ref[...]).astype(BF16)
    qs_ref[0] = (_dot(xn, w_ref[:, 0:w]) * sb_scale).astype(BF16)
    qd_ref[0] = (_rope_lanes(_dot(xn, w_ref[:, 3 * w:4 * w]), rope_ref) * diff_scale).astype(BF16)
    vd = _dot(xn, w_ref[:, 5 * w:6 * w])
    vd_ref[0] = vd
    vdb_ref[0] = vd.astype(BF16)
    n_blk, _, tk = ksb_ref.shape[1:]
    for c0, rope, f_ref, b_ref in ((w, False, kst_ref, ksb_ref), (2 * w, False, vst_ref, vsb_ref),
                                   (4 * w, True, kdt_ref, kdb_ref)):
        ut = _dot_nt(wt_ref[c0:c0 + w, :], xn)
        if rope:
            ut = _rope_rows(ut, rope_t_ref)
        f_ref[0] = ut
        for j in range(n_blk):
            b_ref[0, j] = ut[:, j * tk:(j + 1) * tk].astype(BF16)


def _prompt_in_proj(x, g, w_bf16, wt_bf16, rope_tab, rope_t, w, sb_scale, diff_scale):
    b, t, d = x.shape
    tm = min(TOKEN_TILE, t)
    tk = ATTN_TILE
    assert t % tm == 0 and tm % tk == 0
    tok = lambda dt: (jax.ShapeDtypeStruct((b, t, w), dt), pl.BlockSpec((1, tm, w), lambda bi, i: (bi, i, 0)))
    feat = (jax.ShapeDtypeStruct((b, w, t), F32), pl.BlockSpec((1, w, tm), lambda bi, i: (bi, 0, i)))
    blk = (jax.ShapeDtypeStruct((b, t // tk, w, tk), BF16),
           pl.BlockSpec((1, tm // tk, w, tk), lambda bi, i: (bi, i, 0, 0)))
    outs = [tok(BF16), tok(BF16), tok(F32), tok(BF16), feat, blk, feat, blk, feat, blk]
    return pl.pallas_call(
        functools.partial(_prompt_in_proj_kernel, w=w, sb_scale=sb_scale, diff_scale=diff_scale),
        grid=(b, t // tm),
        in_specs=[pl.BlockSpec((1, tm, d), lambda bi, i: (bi, i, 0)), _resident((1, d)),
                  _resident(w_bf16.shape), _resident(wt_bf16.shape),
                  pl.BlockSpec((tm, 3 * LANES), lambda bi, i: (i, 0)),
                  pl.BlockSpec((ROPE_DIM, tm), lambda bi, i: (0, i))],
        out_specs=[o[1] for o in outs], out_shape=[o[0] for o in outs],
        compiler_params=_params(("parallel", "parallel")), name="prompt_in_proj",
    )(x, g.reshape(1, d), w_bf16, wt_bf16, rope_tab, rope_t)


def _norm_matmul_kernel(*refs, segs, has_rope):
    x_ref, g_ref, w_ref = refs[:3]
    rope_ref = refs[3] if has_rope else None
    outs = refs[4:] if has_rope else refs[3:]
    xn = _rms(x_ref[...], g_ref[...]).astype(BF16)
    oi = 0
    for c0, width, scale, rope, want_f32, want_bf16 in segs:
        u = _dot(xn, w_ref[:, c0:c0 + width])
        if rope:
            u = _rope_lanes(u, rope_ref)
        if want_f32:
            outs[oi][...] = u
            oi += 1
        if want_bf16:
            outs[oi][...] = (u * scale).astype(BF16)
            oi += 1


def _norm_matmul(x, g, w_bf16, segs, rope_tab=None):
    m, d = x.shape
    tm = min(TOKEN_TILE, m)
    assert m % tm == 0
    in_specs = [pl.BlockSpec((tm, d), lambda i: (i, 0)), _resident((1, d)), _resident(w_bf16.shape)]
    args = [x, g.reshape(1, d), w_bf16]
    if rope_tab is not None:
        nrt = rope_tab.shape[0] // tm
        assert rope_tab.shape[0] % tm == 0
        in_specs.append(pl.BlockSpec((tm, 3 * LANES), lambda i: (i % nrt, 0)))
        args.append(rope_tab)
    out_shape, out_specs = [], []
    for _, width, _, _, want_f32, want_bf16 in segs:
        for want, dt in ((want_f32, F32), (want_bf16, BF16)):
            if want:
                out_shape.append(jax.ShapeDtypeStruct((m, width), dt))
                out_specs.append(pl.BlockSpec((tm, width), lambda i: (i, 0)))
    return pl.pallas_call(
        functools.partial(_norm_matmul_kernel, segs=tuple(segs), has_rope=rope_tab is not None),
        grid=(m // tm,), in_specs=in_specs, out_specs=out_specs, out_shape=out_shape,
        compiler_params=_params(("parallel",)), name="norm_matmul",
    )(*args)


def _rope_table(pos):
    half = ROPE_DIM // 2
    inv_freq = ROPE_THETA ** (-jnp.arange(half, dtype=F32) * 2.0 / ROPE_DIM)
    ang = pos.astype(F32)[:, None] * inv_freq[None, :]
    cos, sin = jnp.cos(ang), jnp.sin(ang)
    n = pos.shape[0]
    ones = jnp.ones((n, 64 - ROPE_DIM), F32)
    zeros = jnp.zeros((n, 64 - ROPE_DIM), F32)
    zh = jnp.zeros((n, half), F32)
    c64 = jnp.concatenate([cos, cos, ones], axis=1)
    lo64 = jnp.concatenate([-sin, zh, zeros], axis=1)
    hi64 = jnp.concatenate([zh, sin, zeros], axis=1)
    rep = LANES // 64
    return jnp.concatenate([jnp.tile(c64, (1, rep)), jnp.tile(lo64, (1, rep)), jnp.tile(hi64, (1, rep))], axis=1)


def _rope_table_t(pos):
    half = ROPE_DIM // 2
    inv_freq = ROPE_THETA ** (-jnp.arange(half, dtype=F32) * 2.0 / ROPE_DIM)
    ang = pos.astype(F32)[:, None] * inv_freq[None, :]
    return jnp.concatenate([jnp.cos(ang).T, jnp.sin(ang).T], axis=0)


def _matmul_residual_kernel(a_ref, w_ref, r_ref, o_ref):
    o_ref[...] = r_ref[...] + _dot(a_ref[...].astype(BF16), w_ref[...])


def _matmul_residual(a, w_bf16, res):
    m, k = a.shape
    n = w_bf16.shape[1]
    tm = min(TOKEN_TILE, m)
    assert m % tm == 0
    return pl.pallas_call(
        _matmul_residual_kernel, grid=(m // tm,),
        in_specs=[pl.BlockSpec((tm, k), lambda i: (i, 0)), _resident(w_bf16.shape),
                  pl.BlockSpec((tm, n), lambda i: (i, 0))],
        out_specs=pl.BlockSpec((tm, n), lambda i: (i, 0)),
        out_shape=jax.ShapeDtypeStruct((m, n), F32),
        compiler_params=_params(("parallel",)), name="matmul_residual",
    )(a, w_bf16, res)


def _diff_lambda(lam_ref, lambda_init):
    a = jnp.sum(lam_ref[0:1, :] * lam_ref[1:2, :], axis=-1, keepdims=True)
    b = jnp.sum(lam_ref[2:3, :] * lam_ref[3:4, :], axis=-1, keepdims=True)
    return jnp.exp(a) - jnp.exp(b) + lambda_init


def _sb_weights(s, run, tri, mask):
    n = s.shape[1]
    blk = tri.shape[1]
    nb = n // blk
    log1m = _log2_one_minus_sigmoid(s)
    log_sig = s + log1m
    if mask is not None:
        log1m = jnp.where(mask, log1m, 0.0)
    stacked = jnp.concatenate([log1m[:, b * blk:(b + 1) * blk] for b in range(nb)], axis=0)
    loc = _dot(_split_hi_lo(stacked), tri)
    rows = s.shape[0]
    after = []
    for b in reversed(range(nb)):
        after.append(loc[b * rows:(b + 1) * rows] + _lane_tile(run, blk))
        run = run + jnp.sum(log1m[:, b * blk:(b + 1) * blk], axis=-1, keepdims=True)
    after = jnp.concatenate(after[::-1], axis=1) if nb > 1 else after[0]
    a = jnp.exp2(log_sig + after)
    if mask is not None:
        a = jnp.where(mask, a, 0.0)
    return a, run


def _lane_tile(x, n):
    reps = n // x.shape[1]
    return x if reps == 1 else jnp.concatenate([x] * reps, axis=1)


def _softmax_step(s, m_prev, l_prev):
    m_new = jnp.maximum(m_prev, jnp.max(s, axis=-1, keepdims=True))
    alpha = jnp.exp2(m_prev - m_new)
    p = jnp.exp2(s - _lane_tile(m_new, s.shape[1]))
    l_new = alpha * l_prev + jnp.sum(p, axis=-1, keepdims=True)
    return p, alpha, m_new, l_new


def _prompt_step(i, j, qs_ref, qd_ref, ks_ref, vs_ref, kd_ref, vd_ref, lam_ref, sub_ref, o_ref,
                 q2_ref, acc_ref, st0_ref, st1_ref, *, tq, lambda_init, n_pairs):
    rows = 2 * tq
    tri = _suffix_sum_matrix(tq)

    def init():
        lane = lax.broadcasted_iota(jnp.int32, (rows, LANES), 1)
        row = lax.broadcasted_iota(jnp.int32, (rows, LANES), 0)
        keep = (lane < 64) == (row < tq)
        for p in range(n_pairs):
            for c, q_ref in ((p, qs_ref), (n_pairs + p, qd_ref)):
                qp = q_ref[0, :, p * LANES:(p + 1) * LANES]
                q2 = jnp.concatenate([qp, qp], axis=0).astype(F32)
                q2_ref[c] = jnp.where(keep, q2, 0.0).astype(BF16)
        acc_ref[...] = jnp.zeros_like(acc_ref)
        st0_ref[0:n_pairs] = jnp.zeros((n_pairs, rows, LANES), F32)
        st0_ref[n_pairs:2 * n_pairs] = jnp.full((n_pairs, rows, LANES), NEG, F32)
        st1_ref[...] = jnp.zeros_like(st1_ref)

    qrow = lax.broadcasted_iota(jnp.int32, (rows, tq), 0)
    qpos = jnp.where(qrow < tq, qrow, qrow - tq) + (i - j) * tq
    kpos = lax.broadcasted_iota(jnp.int32, (rows, tq), 1)
    mask_sb = kpos < qpos
    mask_diff = kpos <= qpos

    def main(pairs):
        for p in pairs:
            cols = slice(p * LANES, (p + 1) * LANES)
            a, run = _sb_weights(_dot(q2_ref[p], ks_ref[0, 0, cols, :]), st0_ref[p], tri, mask_sb)
            acc_ref[p] += _dot_nt(a.astype(BF16), vs_ref[0, 0, cols, :])
            st0_ref[p] = run
            c = n_pairs + p
            s = jnp.where(mask_diff, _dot(q2_ref[c], kd_ref[0, 0, cols, :]), NEG)
            pr, alpha, m_new, l_new = _softmax_step(s, st0_ref[c], st1_ref[p])
            acc_ref[c] = alpha * acc_ref[c] + _dot(pr.astype(BF16), vd_ref[0, :, cols])
            st0_ref[c] = m_new
            st1_ref[p] = l_new

    def fin():
        lam = _diff_lambda(lam_ref, lambda_init)
        first_chunk = lax.broadcasted_iota(jnp.int32, (tq, LANES), 1) < 64
        for p in range(n_pairs):
            acc = acc_ref[p]
            o_ref[0, :, p * LANES:(p + 1) * LANES] = jnp.where(
                first_chunk, acc[:tq], acc[tq:]).astype(o_ref.dtype)
            w = acc_ref[n_pairs + p] / st1_ref[p]
            od = _rms(w[:tq] - lam * w[tq:], sub_ref[...]) * (1.0 - lambda_init)
            o_ref[0, :, (n_pairs + p) * LANES:(n_pairs + p + 1) * LANES] = od.astype(o_ref.dtype)

    parts = [functools.partial(main, (p,)) for p in range(n_pairs)]
    return (j == i, init), parts, (j == 0, fin)


def _sample_step(c, active, caches, qs_ref, qd_ref, nks_ref, nvs_ref, nkd_ref, nvd_ref, lam_ref,
                 sub_ref, o_ref, acc_s_ref, run_ref, acc_d_ref, m_ref, l_ref, *, ts, n_chunks, page,
                 lambda_init):
    rows, w = acc_s_ref.shape
    n_chunk_heads = rows // ts
    tri = _suffix_sum_matrix(MXU_DIM)

    r_w = lax.broadcasted_iota(jnp.int32, (rows, w), 0)
    c_w = lax.broadcasted_iota(jnp.int32, (rows, w), 1)
    chunk_of_row = r_w % n_chunk_heads
    q_keep = (c_w // 64) == chunk_of_row

    def block_diag_q(q_ref):
        return jnp.where(q_keep, q_ref[0].astype(F32), 0.0).astype(BF16)

    q_sb = block_diag_q(qs_ref)
    q_d = block_diag_q(qd_ref)

    def sb_chunk(k, v, mask, tri_blk, feature_major):
        s = _dot(q_sb, k) if feature_major else _dot_nt(q_sb, k)
        a, run = _sb_weights(s, run_ref[...], tri_blk, mask)
        a = a.astype(BF16)
        acc_s_ref[...] += _dot_nt(a, v) if feature_major else _dot(a, v)
        run_ref[...] = run

    n_dh = w // LANES
    t_i = lax.broadcasted_iota(jnp.int32, (page, n_dh * page), 0)
    j_i = lax.broadcasted_iota(jnp.int32, (page, n_dh * page), 1)
    spread = jnp.where(j_i // n_dh == t_i, 1.0, 0.0).astype(BF16)
    own_head = (lax.broadcasted_iota(jnp.int32, (rows, n_dh * page), 1) % n_dh
                == (lax.broadcasted_iota(jnp.int32, (rows, n_dh * page), 0) % n_chunk_heads) // 2)

    def diff_chunk(k, v_rows, mask, feature_major):
        s = _dot(q_d, k) if feature_major else _dot_nt(q_d, k)
        if mask is not None:
            s = jnp.where(mask, s, NEG)
        pr, alpha, m_new, l_new = _softmax_step(s, m_ref[...], l_ref[...])
        pr = pr.astype(BF16)
        n_pg = pr.shape[1] // page
        stacked = jnp.concatenate([pr[:, i * page:(i + 1) * page] for i in range(n_pg)], axis=0)
        wide = _dot(stacked, spread)
        wide = jnp.concatenate(
            [jnp.where(own_head, wide[i * rows:(i + 1) * rows], 0.0) for i in range(n_pg)], axis=1)
        acc_d_ref[...] = alpha * acc_d_ref[...] + _dot(wide.astype(BF16), v_rows)
        m_ref[...] = m_new
        l_ref[...] = l_new

    def init():
        acc_s_ref[...] = jnp.zeros_like(acc_s_ref)
        run_ref[...] = jnp.zeros_like(run_ref)
        acc_d_ref[...] = jnp.zeros_like(acc_d_ref)
        m_ref[...] = jnp.full_like(m_ref, NEG)
        l_ref[...] = jnp.zeros_like(l_ref)
        def padded(ref, n_rows):
            x = ref[0]
            pad = jnp.zeros((n_rows - x.shape[0], x.shape[1]), x.dtype)
            return jnp.concatenate([x, pad], axis=0).astype(BF16)
        key = lax.broadcasted_iota(jnp.int32, (rows, page), 1)
        qpos = lax.broadcasted_iota(jnp.int32, (rows, page), 0) // n_chunk_heads
        sb_chunk(padded(nks_ref, page), padded(nvs_ref, page), key < qpos, _suffix_sum_matrix(page), False)
        diff_chunk(padded(nkd_ref, page), padded(nvd_ref, n_dh * page), key <= qpos, False)

    def gathered(page_refs, axis):
        return jnp.concatenate([r[...].astype(BF16) for r in page_refs], axis=axis)

    parts = [lambda: sb_chunk(gathered(caches[0], 1), gathered(caches[1], 1), None, tri, True),
             lambda: diff_chunk(gathered(caches[2], 1), gathered(caches[3], 0), None, True)]

    def fin():
        lam = _diff_lambda(lam_ref, lambda_init)
        o_sb = jnp.where(q_keep, acc_s_ref[...], 0.0)
        r_d = lax.broadcasted_iota(jnp.int32, acc_d_ref.shape, 0) % n_chunk_heads
        coef = jnp.where(r_d % 2 == 0, 1.0, -lam) / l_ref[...]
        o_d = acc_d_ref[...] * coef

        def per_query(x):
            return jnp.concatenate(
                [jnp.sum(x[q * n_chunk_heads:(q + 1) * n_chunk_heads], axis=0, keepdims=True)
                 for q in range(ts)], axis=0)

        o_sb = per_query(o_sb)
        normed = [_rms(per_query(jnp.where(r_d // 2 == h, o_d, 0.0)), sub_ref[...]) * (1.0 - lambda_init)
                  for h in range(n_dh)]
        o_ref[0] = jnp.concatenate([o_sb] + normed, axis=1).astype(o_ref.dtype)

    return ((jnp.logical_and(active, c == 0), init), parts,
            (jnp.logical_and(active, c == n_chunks - 1), fin))


def _mixer_attn_kernel(pt_ref, pb_ref, pi_ref, pj_ref, *refs, pps, ts, n_chunks, page, lambda_init,
                       tq, n_pairs, n_sample_steps):
    del pt_ref, pb_ref
    s = pl.program_id(0)
    caches = [refs[c * pps:(c + 1) * pps] for c in range(4)]
    rest = refs[4 * pps:]
    sample_in, lam_ref, sub_ref, prompt_in = rest[:6], rest[6], rest[7], rest[8:14]
    o_s_ref, o_p_ref = rest[14:16]
    sample_scr, prompt_scr = rest[16:21], rest[21:25]
    sample = _sample_step(s % n_chunks, s < n_sample_steps, caches, *sample_in, lam_ref, sub_ref,
                          o_s_ref, *sample_scr, ts=ts, n_chunks=n_chunks, page=page,
                          lambda_init=lambda_init)
    prompt = _prompt_step(pi_ref[s], pj_ref[s], *prompt_in, lam_ref, sub_ref, o_p_ref, *prompt_scr,
                          tq=tq, lambda_init=lambda_init, n_pairs=n_pairs)
    for cond, fn in (sample[0], prompt[0]):
        pl.when(cond)(fn)
    s_parts, p_parts = sample[1], prompt[1]
    per = -(-len(p_parts) // len(s_parts))
    for k, s_part in enumerate(s_parts):
        s_part()
        for p_part in p_parts[k * per:(k + 1) * per]:
            p_part()
    for cond, fn in (sample[2], prompt[2]):
        pl.when(cond)(fn)


def _mixer_attention(page_table, caches, layer, qs_exp, qd_exp, new_kv, prompt_qkv, lam_vecs, subln,
                     lambda_init):
    bs, n_pages = page_table.shape
    w, page = caches[0].shape[2], caches[0].shape[3]
    rows = qs_exp.shape[1]
    ts = rows // (w // 64)
    pps = PAGES_PER_STEP
    assert n_pages % pps == 0 and (pps * page) % MXU_DIM == 0 and MXU_DIM % page == 0
    n_chunks = n_pages // pps
    n_sample_steps = bs * n_chunks
    pqs, pqd, pks, pvs, pkd, pvd = prompt_qkv
    bp, t, _ = pqs.shape
    tq = ATTN_TILE
    n_pairs = w // LANES
    assert t % tq == 0 and w % LANES == 0 and pks.shape == (bp, t // tq, w, tq)
    sched = [(b, i, j) for b in range(bp) for i in range(t // tq) for j in range(i, -1, -1)]
    n_steps = len(sched)
    assert n_steps >= n_sample_steps
    pb, pi, pj = (jnp.asarray([e[k] for e in sched], jnp.int32) for k in range(3))

    def sample_pos(s):
        s = jnp.minimum(s, n_sample_steps - 1)
        return s // n_chunks, s % n_chunks

    def page_spec(i, shape):
        def imap(s, pt, pb, pi, pj):
            b, c = sample_pos(s)
            return (layer, pt[b * n_pages + (n_chunks - 1 - c) * pps + i], 0, 0)
        return pl.BlockSpec((None, None) + shape, imap)

    in_specs, args = [], []
    for cache in caches:
        for i in range(pps):
            in_specs.append(page_spec(i, cache.shape[2:]))
            args.append(cache)
    per_b = lambda shape: pl.BlockSpec((1,) + shape, lambda s, pt, pb, pi, pj: (sample_pos(s)[0], 0, 0))
    in_specs += [per_b((rows, w)), per_b((rows, w))] + [per_b(a.shape[1:]) for a in new_kv]
    args += [qs_exp, qd_exp] + list(new_kv)
    in_specs += [pl.BlockSpec(lam_vecs.shape, lambda s, pt, pb, pi, pj: (0, 0)),
                 pl.BlockSpec(subln.shape, lambda s, pt, pb, pi, pj: (0, 0))]
    args += [lam_vecs, subln]
    qspec = pl.BlockSpec((1, tq, w), lambda s, pt, pb, pi, pj: (pb[s], pi[s], 0))
    fspec = pl.BlockSpec((1, 1, w, tq), lambda s, pt, pb, pi, pj: (pb[s], pj[s], 0, 0))
    vspec = pl.BlockSpec((1, tq, w), lambda s, pt, pb, pi, pj: (pb[s], pj[s], 0))
    in_specs += [qspec, qspec, fspec, fspec, fspec, vspec]
    args += [pqs, pqd, pks, pvs, pkd, pvd]
    grid_spec = pltpu.PrefetchScalarGridSpec(
        num_scalar_prefetch=4, grid=(n_steps,), in_specs=in_specs,
        out_specs=[pl.BlockSpec((1, ts, 2 * w), lambda s, pt, pb, pi, pj: (sample_pos(s)[0], 0, 0)),
                   pl.BlockSpec((1, tq, 2 * w), lambda s, pt, pb, pi, pj: (pb[s], pi[s], 0))],
        scratch_shapes=[pltpu.VMEM((rows, w), F32), pltpu.VMEM((rows, LANES), F32),
                        pltpu.VMEM((rows, LANES), F32), pltpu.VMEM((rows, LANES), F32),
                        pltpu.VMEM((rows, LANES), F32),
                        pltpu.VMEM((2 * n_pairs, 2 * tq, LANES), BF16),
                        pltpu.VMEM((2 * n_pairs, 2 * tq, LANES), F32),
                        pltpu.VMEM((2 * n_pairs, 2 * tq, LANES), F32),
                        pltpu.VMEM((n_pairs, 2 * tq, LANES), F32)])
    return pl.pallas_call(
        functools.partial(_mixer_attn_kernel, pps=pps, ts=ts, n_chunks=n_chunks, page=page,
                          lambda_init=lambda_init, tq=tq, n_pairs=n_pairs,
                          n_sample_steps=n_sample_steps),
        grid_spec=grid_spec,
        out_shape=[jax.ShapeDtypeStruct((bs, ts, 2 * w), BF16),
                   jax.ShapeDtypeStruct((bp, t, 2 * w), BF16)],
        compiler_params=_params(("arbitrary",)), name="mixer_attention",
    )(page_table.reshape(-1), pb, pi, pj, *args)


def _mem_attn_kernel(q_ref, mk_ref, mv_ref, o_ref, *, n_heads):
    hd = q_ref.shape[2] // n_heads
    for h in range(n_heads):
        cols = slice(h * hd, (h + 1) * hd)
        o = _softmax_attend(q_ref[0, :, cols], mk_ref[0, :, cols].astype(BF16),
                            mv_ref[0, :, cols].astype(BF16))
        o_ref[0, :, cols] = o.astype(o_ref.dtype)


def _softmax_attend(q, k, v):
    s = _dot_nt(q, k)
    p = jnp.exp(s - jnp.max(s, axis=-1, keepdims=True))
    return _dot(p.astype(BF16), v) / jnp.sum(p, axis=-1, keepdims=True)


def _mem_attention(q, mk, mv, n_heads):
    b, t, w = q.shape
    tm = min(TOKEN_TILE, t)
    assert t % tm == 0
    m = mk.shape[1]
    return pl.pallas_call(
        functools.partial(_mem_attn_kernel, n_heads=n_heads), grid=(b, t // tm),
        in_specs=[pl.BlockSpec((1, tm, w), lambda bi, i: (bi, i, 0)),
                  pl.BlockSpec((1, m, w), lambda bi, i: (bi, 0, 0)),
                  pl.BlockSpec((1, m, w), lambda bi, i: (bi, 0, 0))],
        out_specs=pl.BlockSpec((1, tm, w), lambda bi, i: (bi, i, 0)),
        out_shape=jax.ShapeDtypeStruct((b, t, w), BF16),
        compiler_params=_params(("parallel", "arbitrary")), name="mem_attention",
    )(q, mk, mv)


def _mixer_out_mem_kernel(o_ref, h_ref, wout_ref, g_ref, wq_ref, mk_ref, mv_ref, wo_ref, out_ref,
                          om_ref, *, n_heads, scale):
    h1 = h_ref[0] + _dot(o_ref[0], wout_ref[...])
    q = (_dot(_rms(h1, g_ref[...]).astype(BF16), wq_ref[...]) * scale).astype(BF16)
    hd = q.shape[1] // n_heads
    for h in range(n_heads):
        cols = slice(h * hd, (h + 1) * hd)
        om_ref[:, cols] = _softmax_attend(q[:, cols], mk_ref[0, :, cols].astype(BF16),
                                          mv_ref[0, :, cols].astype(BF16)).astype(BF16)
    out_ref[0] = h1 + _dot(om_ref[...], wo_ref[...])


def _mixer_out_mem(o, h, w_out, g_mem, w_q, mk, mv, w_o, n_heads, scale):
    b, t, d = h.shape
    tm = min(TOKEN_TILE, t)
    assert t % tm == 0
    m, w = mk.shape[1:]
    tile = lambda last: pl.BlockSpec((1, tm, last), lambda bi, i: (bi, i, 0))
    return pl.pallas_call(
        functools.partial(_mixer_out_mem_kernel, n_heads=n_heads, scale=scale), grid=(b, t // tm),
        in_specs=[tile(o.shape[2]), tile(d), _resident(w_out.shape), _resident((1, d)),
                  _resident(w_q.shape), pl.BlockSpec((1, m, w), lambda bi, i: (bi, 0, 0)),
                  pl.BlockSpec((1, m, w), lambda bi, i: (bi, 0, 0)), _resident(w_o.shape)],
        out_specs=tile(d), out_shape=jax.ShapeDtypeStruct((b, t, d), F32),
        scratch_shapes=[pltpu.VMEM((tm, w), BF16)],
        compiler_params=_params(("parallel", "arbitrary")), name="mixer_out_mem",
    )(o, h, w_out, g_mem.reshape(1, d), w_q, mk, mv, w_o)


def _ffn_kernel(h_ref, cin_ref, g_ref, wup_ref, wconv_ref, bconv_ref, wdown_ref, gfin_ref,
                y_ref, cout_ref, ubuf_g_ref, ubuf_v_ref, act_ref, *, shift, dff, final_norm):
    t = pl.program_id(1)
    tm = h_ref.shape[1]
    cr = cout_ref.shape[1]

    @pl.when(t == 0)
    def _():
        cout_ref[0] = cin_ref[0]

    h = h_ref[0]
    hn = _rms(h, g_ref[...]).astype(BF16)

    def conv(ubuf_ref, c0):
        cols = slice(c0, c0 + FF_CHUNK)
        u = _dot(hn, wup_ref[:, cols])
        ubuf_ref[0:cr, :] = cout_ref[0, :, cols]
        ubuf_ref[cr:cr + tm, :] = u
        out = bconv_ref[:, cols] + u * wconv_ref[CONV_W - 1:CONV_W, cols]
        for i in range(CONV_W - 1):
            back = (CONV_W - 1 - i) * shift
            out = out + ubuf_ref[cr - back:cr - back + tm, :] * wconv_ref[i:i + 1, cols]
        cout_ref[0, :, cols] = ubuf_ref[tm:tm + cr, :]
        return out

    for j in range(dff // FF_CHUNK):
        gate = conv(ubuf_g_ref, j * FF_CHUNK)
        val = conv(ubuf_v_ref, dff + j * FF_CHUNK)
        act = gate * (1.0 / (1.0 + jnp.exp(-gate))) * val
        act_ref[:, j * FF_CHUNK:(j + 1) * FF_CHUNK] = act.astype(BF16)

    out = h + _dot(act_ref[...], wdown_ref[...])
    y_ref[0] = _rms(out, gfin_ref[...]) if final_norm else out


def _conv_ffn(h, conv_in, g, w_up, w_conv, b_conv, w_down, g_final, final_norm, shift):
    ngrp, t, d = h.shape
    dff = w_down.shape[0]
    cr = conv_in.shape[1]
    tm = min(TOKEN_TILE, t)
    assert t % tm == 0 and dff % FF_CHUNK == 0 and cr >= (CONV_W - 1) * shift and tm >= cr
    return pl.pallas_call(
        functools.partial(_ffn_kernel, shift=shift, dff=dff, final_norm=final_norm),
        grid=(ngrp, t // tm),
        in_specs=[pl.BlockSpec((1, tm, d), lambda gi, i: (gi, i, 0)),
                  pl.BlockSpec((1, cr, 2 * dff), lambda gi, i: (gi, 0, 0),
                               pipeline_mode=pl.Buffered(1)),
                  _resident((1, d)), _resident(w_up.shape), _resident(w_conv.shape),
                  _resident((1, 2 * dff)), _resident(w_down.shape), _resident((1, d))],
        out_specs=[pl.BlockSpec((1, tm, d), lambda gi, i: (gi, i, 0)),
                   pl.BlockSpec((1, cr, 2 * dff), lambda gi, i: (gi, 0, 0))],
        out_shape=[jax.ShapeDtypeStruct((ngrp, t, d), F32),
                   jax.ShapeDtypeStruct((ngrp, cr, 2 * dff), F32)],
        scratch_shapes=[pltpu.VMEM((cr + tm, FF_CHUNK), F32), pltpu.VMEM((cr + tm, FF_CHUNK), F32),
                        pltpu.VMEM((tm, dff), BF16)],
        compiler_params=_params(("parallel", "arbitrary")), name="conv_ffn",
    )(h, conv_in, g.reshape(1, d), w_up, w_conv, b_conv.reshape(1, 2 * dff), w_down,
      g_final.reshape(1, d))


def kernel(x_prompt, x_sample, cache_sb_k, cache_sb_v, cache_diff_k, cache_diff_v, cache_mem_k, cache_mem_v, state_conv, page_table, mem_prompt, norm_mix, w_in, lambda_q1, lambda_k1, lambda_q2, lambda_k2, diff_subln, w_out, norm_mem_q, norm_mem_kv, w_mem_q, w_mem_k, w_mem_v, w_mem_o, norm_ffn, w_up, w_conv, b_conv, w_down, norm_final):
    bp, tp, d = x_prompt.shape
    bs, ts, _ = x_sample.shape
    depth, n_pool, page, sb_heads, sb_hd = cache_sb_k.shape
    diff_heads, diff_hd = cache_diff_k.shape[3], cache_diff_k.shape[5]
    mem_tokens, mem_heads, mem_hd = cache_mem_k.shape[2:]
    n_pages = page_table.shape[1]
    past_len = n_pages * page
    sb_w = sb_heads * sb_hd
    dqk_w = diff_heads * 2 * diff_hd
    dv_w = diff_heads * 2 * diff_hd
    mem_w = mem_heads * mem_hd
    dff = w_down.shape[1]
    assert sb_hd == 64 and diff_hd == 64 and sb_w == dqk_w == dv_w and ts <= SUBLANES
    w = sb_w
    n_chunk_heads = w // 64

    sb_scale = math.log2(math.e) / math.sqrt(sb_hd)
    diff_scale = math.log2(math.e) / math.sqrt(diff_hd)
    mem_scale = 1.0 / math.sqrt(mem_hd)

    in_segs = [(0, w, sb_scale, False, False, True), (w, w, 1.0, False, True, True),
               (2 * w, w, 1.0, False, True, True), (3 * w, w, diff_scale, True, False, True),
               (4 * w, w, 1.0, True, True, True), (5 * w, w, 1.0, False, True, True)]
    rope_p = _rope_table(jnp.arange(tp, dtype=jnp.int32))
    rope_pt = _rope_table_t(jnp.arange(tp, dtype=jnp.int32))
    rope_s = jnp.tile(_rope_table(past_len + jnp.arange(ts, dtype=jnp.int32)), (bs, 1))

    hp = x_prompt.reshape(bp * tp, d)
    hs = x_sample.reshape(bs * ts, d)
    outs = {k: [] for k in ("p_sbk", "p_sbv", "p_dk", "p_dv", "p_mk", "p_mv", "p_conv",
                            "s_sbk", "s_sbv", "s_dk", "s_dv", "s_conv")}
    conv_rows_p = SUBLANES
    for l in range(depth):
        lambda_init = 0.8 - 0.6 * math.exp(-0.3 * l)
        lam_vecs = jnp.stack([lambda_q1[l], lambda_k1[l], lambda_q2[l], lambda_k2[l]]).astype(F32)
        subln = diff_subln[l].reshape(1, -1)
        w_in_b = w_in[l].astype(BF16)
        w_out_b = w_out[l].astype(BF16)
        w_q_b = w_mem_q[l].astype(BF16)
        w_kv_b = jnp.concatenate([w_mem_k[l], w_mem_v[l]], axis=1).astype(BF16)
        w_o_b = w_mem_o[l].astype(BF16)
        w_up_b = w_up[l].astype(BF16)
        w_down_b = w_down[l].astype(BF16)
        q_seg = [(0, mem_w, mem_scale, False, False, True)]

        def mem_block(h2d, ngrp, mk, mv):
            (q,) = _norm_matmul(h2d, norm_mem_q[l], w_q_b, q_seg)
            t = h2d.shape[0] // ngrp
            o = _mem_attention(q.reshape(ngrp, t, mem_w), mk, mv, mem_heads)
            return _matmul_residual(o.reshape(ngrp * t, mem_w), w_o_b, h2d)

        qs, qd, vdf, vdb, kst, ksb, vst, vsb, kdt, kdb = _prompt_in_proj(
            hp.reshape(bp, tp, d), norm_mix[l], w_in_b, w_in[l].T.astype(BF16), rope_p, rope_pt, w,
            sb_scale, diff_scale)
        s_qs, s_ksf, _, s_vsf, _, s_qd, s_kdf, _, s_vdf, _ = _norm_matmul(
            hs, norm_mix[l], w_in_b, in_segs, rope_s)
        expand = lambda q: jnp.repeat(q.reshape(bs, ts, w), n_chunk_heads, axis=1)
        pad_new = lambda a: jnp.pad(a.reshape(bs, ts, w), ((0, 0), (0, SUBLANES - ts), (0, 0)))
        caches = [jnp.moveaxis(cache_sb_k, 2, -1).reshape(depth, n_pool, w, page),
                  jnp.moveaxis(cache_sb_v, 2, -1).reshape(depth, n_pool, w, page),
                  jnp.moveaxis(cache_diff_k, 2, -1).reshape(depth, n_pool, w, page),
                  cache_diff_v.reshape(depth, n_pool, page * diff_heads, 2 * diff_hd)]
        new_kv = [pad_new(a) for a in (s_ksf, s_vsf, s_kdf, s_vdf)]
        new_kv[3] = new_kv[3].reshape(bs, SUBLANES * diff_heads, 2 * diff_hd)
        o_s, o = _mixer_attention(page_table, caches, l, expand(s_qs), expand(s_qd), new_kv,
                                  (qs, qd, ksb, vsb, kdb, vdb), lam_vecs, subln, lambda_init)

        mkf, mvf = _norm_matmul(mem_prompt.reshape(bp * mem_tokens, d), norm_mem_kv[l], w_kv_b,
                                [(0, mem_w, 1.0, False, True, False),
                                 (mem_w, mem_w, 1.0, False, True, False)])
        mk3 = mkf.reshape(bp, mem_tokens, mem_w)
        mv3 = mvf.reshape(bp, mem_tokens, mem_w)
        hp3 = _mixer_out_mem(o, hp.reshape(bp, tp, d), w_out_b, norm_mem_q[l], w_q_b, mk3, mv3, w_o_b,
                             mem_heads, mem_scale)
        conv0 = jnp.zeros((bp, conv_rows_p, 2 * dff), F32)
        last = l == depth - 1
        yp, conv_p = _conv_ffn(hp3, conv0, norm_ffn[l], w_up_b, w_conv[l],
                               b_conv[l], w_down_b, norm_final, last, 1)
        outs["p_sbk"].append(jnp.moveaxis(kst.reshape(bp, sb_heads, sb_hd, tp), -1, 1))
        outs["p_sbv"].append(jnp.moveaxis(vst.reshape(bp, sb_heads, sb_hd, tp), -1, 1))
        outs["p_dk"].append(jnp.moveaxis(kdt.reshape(bp, diff_heads, 2, diff_hd, tp), -1, 1))
        outs["p_dv"].append(vdf.reshape(bp, tp, diff_heads, 2 * diff_hd))
        outs["p_mk"].append(mk3.reshape(bp, mem_tokens, mem_heads, mem_hd))
        outs["p_mv"].append(mv3.reshape(bp, mem_tokens, mem_heads, mem_hd))
        outs["p_conv"].append(conv_p[:, conv_rows_p - (CONV_W - 1):])

        hs = _matmul_residual(o_s.reshape(bs * ts, 2 * w), w_out_b, hs)
        hs = mem_block(hs, bs, cache_mem_k[l].reshape(bs, mem_tokens, mem_w),
                       cache_mem_v[l].reshape(bs, mem_tokens, mem_w))
        hs_tm = hs.reshape(bs, ts, d).transpose(1, 0, 2).reshape(1, ts * bs, d)
        conv_in = state_conv[l].transpose(1, 0, 2).reshape(1, (CONV_W - 1) * bs, 2 * dff)
        ys, conv_s = _conv_ffn(hs_tm, conv_in, norm_ffn[l], w_up_b, w_conv[l], b_conv[l], w_down_b,
                               norm_final, last, bs)
        hs = ys.reshape(ts, bs, d).transpose(1, 0, 2).reshape(bs * ts, d)
        outs["s_sbk"].append(s_ksf.reshape(bs, ts, sb_heads, sb_hd))
        outs["s_sbv"].append(s_vsf.reshape(bs, ts, sb_heads, sb_hd))
        outs["s_dk"].append(s_kdf.reshape(bs, ts, diff_heads, 2, diff_hd))
        outs["s_dv"].append(s_vdf.reshape(bs, ts, diff_heads, 2 * diff_hd))
        outs["s_conv"].append(conv_s.reshape(CONV_W - 1, bs, 2 * dff).transpose(1, 0, 2))
        hp = yp.reshape(bp * tp, d)

    st = lambda xs: jnp.stack(xs, axis=0)
    return (hp.reshape(bp, tp, d), hs.reshape(bs, ts, d), st(outs["p_sbk"]), st(outs["p_sbv"]),
            st(outs["p_dk"]), st(outs["p_dv"]), st(outs["p_mk"]), st(outs["p_mv"]), st(outs["p_conv"]),
            st(outs["s_sbk"]), st(outs["s_sbv"]), st(outs["s_dk"]), st(outs["s_dv"]), st(outs["s_conv"]))
```

```python
import functools
import math

import jax
import jax.numpy as jnp
from jax import lax
from jax.experimental import pallas as pl
from jax.experimental.pallas import tpu as pltpu

F32 = jnp.float32
BF16 = jnp.bfloat16

EPS = 1e-6
ROPE_DIM = 16
ROPE_THETA = 500000.0
CONV_W = 3
NEG = -1e30

LANES = 128
SUBLANES = 8
MXU_DIM = 256
VMEM_LIMIT = 56 * 1024 * 1024

TOKEN_TILE = 512
ATTN_TILE = 256
PAGES_PER_STEP = 16
FF_CHUNK = 256


def _dot(a, b):
    return jnp.dot(a, b, preferred_element_type=F32)


def _dot_nt(a, b):
    return lax.dot_general(a, b, (((1,), (1,)), ((), ())), preferred_element_type=F32)


def _rms(x, g):
    return x * lax.rsqrt(jnp.mean(x * x, axis=-1, keepdims=True) + EPS) * g


def _resident(shape):
    nd = len(shape)
    return pl.BlockSpec(shape, lambda *_: (0,) * nd, pipeline_mode=pl.Buffered(1))


def _params(sem):
    return pltpu.CompilerParams(dimension_semantics=sem, vmem_limit_bytes=VMEM_LIMIT)


def _split_hi_lo(x):
    hi = x.astype(BF16)
    lo = (x - hi.astype(F32)).astype(BF16)
    return jnp.concatenate([hi, lo], axis=1)


def _suffix_sum_matrix(n):
    r = lax.broadcasted_iota(jnp.int32, (2 * n, n), 0)
    c = lax.broadcasted_iota(jnp.int32, (2 * n, n), 1)
    return jnp.where(jnp.where(r >= n, r - n, r) > c, 1.0, 0.0).astype(BF16)


def _log2_one_minus_sigmoid(s2):
    return -(jnp.maximum(s2, 0.0) + jnp.log2(1.0 + jnp.exp2(-jnp.abs(s2))))


def _rope_lanes(u, rope_ref):
    cos = rope_ref[:, 0:LANES]
    sin_lo = rope_ref[:, LANES:2 * LANES]
    sin_hi = rope_ref[:, 2 * LANES:3 * LANES]
    half = ROPE_DIM // 2
    pieces = []
    for j in range(u.shape[1] // LANES):
        uj = u[:, j * LANES:(j + 1) * LANES]
        pieces.append(uj * cos + pltpu.roll(uj, LANES - half, 1) * sin_lo
                      + pltpu.roll(uj, half, 1) * sin_hi)
    return jnp.concatenate(pieces, axis=1)


def _rope_rows(ut, rope_t_ref):
    half = ROPE_DIM // 2
    cos = rope_t_ref[0:half, :]
    sin = rope_t_ref[half:ROPE_DIM, :]
    pieces = []
    for c in range(ut.shape[0] // 64):
        x1 = ut[c * 64:c * 64 + half]
        x2 = ut[c * 64 + half:c * 64 + ROPE_DIM]
        pieces += [x1 * cos - x2 * sin, x2 * cos + x1 * sin, ut[c * 64 + ROPE_DIM:(c + 1) * 64]]
    return jnp.concatenate(pieces, axis=0)


def _prompt_in_proj_kernel(x_ref, g_ref, w_ref, wt_ref, rope_ref, rope_t_ref,
                           qs_ref, qd_ref, vd_ref, vdb_ref, kst_ref, ksb_ref, vst_ref, vsb_ref,
                           kdt_ref, kdb_ref, *, w, sb_scale, diff_scale):
    xn = _rms(x_ref[0], g_ref[...]).astype(BF16)
    qs_ref[0] = (_dot(xn, w_ref[:, 0:w]) * sb_scale).astype(BF16)
    qd_ref[0] = (_rope_lanes(_dot(xn, w_ref[:, 3 * w:4 * w]), rope_ref) * diff_scale).astype(BF16)
    vd = _dot(xn, w_ref[:, 5 * w:6 * w])
    vd_ref[0] = vd
    vdb_ref[0] = vd.astype(BF16)
    n_blk, _, tk = ksb_ref.shape[1:]
    for c0, rope, f_ref, b_ref in ((w, False, kst_ref, ksb_ref), (2 * w, False, vst_ref, vsb_ref),
                                   (4 * w, True, kdt_ref, kdb_ref)):
        ut = _dot_nt(wt_ref[c0:c0 + w, :], xn)
        if rope:
            ut = _rope_rows(ut, rope_t_ref)
        f_ref[0] = ut
        for j in range(n_blk):
            b_ref[0, j] = ut[:, j * tk:(j + 1) * tk].astype(BF16)


def _prompt_in_proj(x, g, w_bf16, wt_bf16, rope_tab, rope_t, w, sb_scale, diff_scale):
    b, t, d = x.shape
    tm = min(TOKEN_TILE, t)
    tk = ATTN_TILE
    assert t % tm == 0 and tm % tk == 0
    tok = lambda dt: (jax.ShapeDtypeStruct((b, t, w), dt), pl.BlockSpec((1, tm, w), lambda bi, i: (bi, i, 0)))
    feat = (jax.ShapeDtypeStruct((b, w, t), F32), pl.BlockSpec((1, w, tm), lambda bi, i: (bi, 0, i)))
    blk = (jax.ShapeDtypeStruct((b, t // tk, w, tk), BF16),
           pl.BlockSpec((1, tm // tk, w, tk), lambda bi, i: (bi, i, 0, 0)))
    outs = [tok(BF16), tok(BF16), tok(F32), tok(BF16), feat, blk, feat, blk, feat, blk]
    return pl.pallas_call(
        functools.partial(_prompt_in_proj_kernel, w=w, sb_scale=sb_scale, diff_scale=diff_scale),
        grid=(b, t // tm),
        in_specs=[pl.BlockSpec((1, tm, d), lambda bi, i: (bi, i, 0)), _resident((1, d)),
                  _resident(w_bf16.shape), _resident(wt_bf16.shape),
                  pl.BlockSpec((tm, 3 * LANES), lambda bi, i: (i, 0)),
                  pl.BlockSpec((ROPE_DIM, tm), lambda bi, i: (0, i))],
        out_specs=[o[1] for o in outs], out_shape=[o[0] for o in outs],
        compiler_params=_params(("parallel", "parallel")), name="prompt_in_proj",
    )(x, g.reshape(1, d), w_bf16, wt_bf16, rope_tab, rope_t)


def _norm_matmul_kernel(*refs, segs, has_rope):
    x_ref, g_ref, w_ref = refs[:3]
    rope_ref = refs[3] if has_rope else None
    outs = refs[4:] if has_rope else refs[3:]
    xn = _rms(x_ref[...], g_ref[...]).astype(BF16)
    oi = 0
    for c0, width, scale, rope, want_f32, want_bf16 in segs:
        u = _dot(xn, w_ref[:, c0:c0 + width])
        if rope:
            u = _rope_lanes(u, rope_ref)
        if want_f32:
            outs[oi][...] = u
            oi += 1
        if want_bf16:
            outs[oi][...] = (u * scale).astype(BF16)
            oi += 1


def _norm_matmul(x, g, w_bf16, segs, rope_tab=None):
    m, d = x.shape
    tm = min(TOKEN_TILE, m)
    assert m % tm == 0
    in_specs = [pl.BlockSpec((tm, d), lambda i: (i, 0)), _resident((1, d)), _resident(w_bf16.shape)]
    args = [x, g.reshape(1, d), w_bf16]
    if rope_tab is not None:
        nrt = rope_tab.shape[0] // tm
        assert rope_tab.shape[0] % tm == 0
        in_specs.append(pl.BlockSpec((tm, 3 * LANES), lambda i: (i % nrt, 0)))
        args.append(rope_tab)
    out_shape, out_specs = [], []
    for _, width, _, _, want_f32, want_bf16 in segs:
        for want, dt in ((want_f32, F32), (want_bf16, BF16)):
            if want:
                out_shape.append(jax.ShapeDtypeStruct((m, width), dt))
                out_specs.append(pl.BlockSpec((tm, width), lambda i: (i, 0)))
    return pl.pallas_call(
        functools.partial(_norm_matmul_kernel, segs=tuple(segs), has_rope=rope_tab is not None),
        grid=(m // tm,), in_specs=in_specs, out_specs=out_specs, out_shape=out_shape,
        compiler_params=_params(("parallel",)), name="norm_matmul",
    )(*args)


def _rope_table(pos):
    half = ROPE_DIM // 2
    inv_freq = ROPE_THETA ** (-jnp.arange(half, dtype=F32) * 2.0 / ROPE_DIM)
    ang = pos.astype(F32)[:, None] * inv_freq[None, :]
    cos, sin = jnp.cos(ang), jnp.sin(ang)
    n = pos.shape[0]
    ones = jnp.ones((n, 64 - ROPE_DIM), F32)
    zeros = jnp.zeros((n, 64 - ROPE_DIM), F32)
    zh = jnp.zeros((n, half), F32)
    c64 = jnp.concatenate([cos, cos, ones], axis=1)
    lo64 = jnp.concatenate([-sin, zh, zeros], axis=1)
    hi64 = jnp.concatenate([zh, sin, zeros], axis=1)
    rep = LANES // 64
    return jnp.concatenate([jnp.tile(c64, (1, rep)), jnp.tile(lo64, (1, rep)), jnp.tile(hi64, (1, rep))], axis=1)


def _rope_table_t(pos):
    half = ROPE_DIM // 2
    inv_freq = ROPE_THETA ** (-jnp.arange(half, dtype=F32) * 2.0 / ROPE_DIM)
    ang = pos.astype(F32)[:, None] * inv_freq[None, :]
    return jnp.concatenate([jnp.cos(ang).T, jnp.sin(ang).T], axis=0)


def _matmul_residual_kernel(a_ref, w_ref, r_ref, o_ref):
    o_ref[...] = r_ref[...] + _dot(a_ref[...].astype(BF16), w_ref[...])


def _matmul_residual(a, w_bf16, res):
    m, k = a.shape
    n = w_bf16.shape[1]
    tm = min(TOKEN_TILE, m)
    assert m % tm == 0
    return pl.pallas_call(
        _matmul_residual_kernel, grid=(m // tm,),
        in_specs=[pl.BlockSpec((tm, k), lambda i: (i, 0)), _resident(w_bf16.shape),
                  pl.BlockSpec((tm, n), lambda i: (i, 0))],
        out_specs=pl.BlockSpec((tm, n), lambda i: (i, 0)),
        out_shape=jax.ShapeDtypeStruct((m, n), F32),
        compiler_params=_params(("parallel",)), name="matmul_residual",
    )(a, w_bf16, res)


def _diff_lambda(lam_ref, lambda_init):
    a = jnp.sum(lam_ref[0:1, :] * lam_ref[1:2, :], axis=-1, keepdims=True)
    b = jnp.sum(lam_ref[2:3, :] * lam_ref[3:4, :], axis=-1, keepdims=True)
    return jnp.exp(a) - jnp.exp(b) + lambda_init


def _sb_weights(s, run, tri, mask):
    n = s.shape[1]
    blk = tri.shape[1]
    nb = n // blk
    log1m = _log2_one_minus_sigmoid(s)
    log_sig = s + log1m
    if mask is not None:
        log1m = jnp.where(mask, log1m, 0.0)
    stacked = jnp.concatenate([log1m[:, b * blk:(b + 1) * blk] for b in range(nb)], axis=0)
    loc = _dot(_split_hi_lo(stacked), tri)
    rows = s.shape[0]
    after = []
    for b in reversed(range(nb)):
        after.append(loc[b * rows:(b + 1) * rows] + _lane_tile(run, blk))
        run = run + jnp.sum(log1m[:, b * blk:(b + 1) * blk], axis=-1, keepdims=True)
    after = jnp.concatenate(after[::-1], axis=1) if nb > 1 else after[0]
    a = jnp.exp2(log_sig + after)
    if mask is not None:
        a = jnp.where(mask, a, 0.0)
    return a, run


def _lane_tile(x, n):
    reps = n // x.shape[1]
    return x if reps == 1 else jnp.concatenate([x] * reps, axis=1)


def _softmax_step(s, m_prev, l_prev):
    m_new = jnp.maximum(m_prev, jnp.max(s, axis=-1, keepdims=True))
    alpha = jnp.exp2(m_prev - m_new)
    p = jnp.exp2(s - _lane_tile(m_new, s.shape[1]))
    l_new = alpha * l_prev + jnp.sum(p, axis=-1, keepdims=True)
    return p, alpha, m_new, l_new


def _prompt_step(i, j, qs_ref, qd_ref, ks_ref, vs_ref, kd_ref, vd_ref, lam_ref, sub_ref, o_ref,
                 q2_ref, acc_ref, st0_ref, st1_ref, *, tq, lambda_init, n_pairs):
    rows = 2 * tq
    tri = _suffix_sum_matrix(tq)

    def init():
        lane = lax.broadcasted_iota(jnp.int32, (rows, LANES), 1)
        row = lax.broadcasted_iota(jnp.int32, (rows, LANES), 0)
        keep = (lane < 64) == (row < tq)
        for p in range(n_pairs):
            for c, q_ref in ((p, qs_ref), (n_pairs + p, qd_ref)):
                qp = q_ref[0, :, p * LANES:(p + 1) * LANES]
                q2 = jnp.concatenate([qp, qp], axis=0).astype(F32)
                q2_ref[c] = jnp.where(keep, q2, 0.0).astype(BF16)
        acc_ref[...] = jnp.zeros_like(acc_ref)
        st0_ref[0:n_pairs] = jnp.zeros((n_pairs, rows, LANES), F32)
        st0_ref[n_pairs:2 * n_pairs] = jnp.full((n_pairs, rows, LANES), NEG, F32)
        st1_ref[...] = jnp.zeros_like(st1_ref)

    def main(diagonal):
        mask_sb = mask_diff = None
        if diagonal:
            qrow = lax.broadcasted_iota(jnp.int32, (rows, tq), 0)
            qpos = jnp.where(qrow < tq, qrow, qrow - tq)
            kpos = lax.broadcasted_iota(jnp.int32, (rows, tq), 1)
            mask_sb = kpos < qpos
            mask_diff = kpos <= qpos
        for p in range(n_pairs):
            cols = slice(p * LANES, (p + 1) * LANES)
            a, run = _sb_weights(_dot(q2_ref[p], ks_ref[0, 0, cols, :]), st0_ref[p], tri, mask_sb)
            acc_ref[p] += _dot_nt(a.astype(BF16), vs_ref[0, 0, cols, :])
            st0_ref[p] = run
            c = n_pairs + p
            s = _dot(q2_ref[c], kd_ref[0, 0, cols, :])
            if diagonal:
                s = jnp.where(mask_diff, s, NEG)
            pr, alpha, m_new, l_new = _softmax_step(s, st0_ref[c], st1_ref[p])
            acc_ref[c] = alpha * acc_ref[c] + _dot(pr.astype(BF16), vd_ref[0, :, cols])
            st0_ref[c] = m_new
            st1_ref[p] = l_new

    def fin():
        lam = _diff_lambda(lam_ref, lambda_init)
        first_chunk = lax.broadcasted_iota(jnp.int32, (tq, LANES), 1) < 64
        for p in range(n_pairs):
            acc = acc_ref[p]
            o_ref[0, :, p * LANES:(p + 1) * LANES] = jnp.where(
                first_chunk, acc[:tq], acc[tq:]).astype(o_ref.dtype)
            w = acc_ref[n_pairs + p] / st1_ref[p]
            od = _rms(w[:tq] - lam * w[tq:], sub_ref[...]) * (1.0 - lambda_init)
            o_ref[0, :, (n_pairs + p) * LANES:(n_pairs + p + 1) * LANES] = od.astype(o_ref.dtype)

    steps = [(j == i, functools.partial(main, True)), (j != i, functools.partial(main, False))]
    return (j == i, init), steps, (j == 0, fin)


def _sample_step(c, active, caches, qs_ref, qd_ref, nks_ref, nvs_ref, nkd_ref, nvd_ref, lam_ref,
                 sub_ref, o_ref, acc_s_ref, run_ref, acc_d_ref, m_ref, l_ref, *, ts, n_chunks, page,
                 lambda_init):
    rows, w = acc_s_ref.shape
    n_chunk_heads = rows // ts
    tri = _suffix_sum_matrix(MXU_DIM)

    r_w = lax.broadcasted_iota(jnp.int32, (rows, w), 0)
    c_w = lax.broadcasted_iota(jnp.int32, (rows, w), 1)
    chunk_of_row = r_w % n_chunk_heads
    q_keep = (c_w // 64) == chunk_of_row

    def block_diag_q(q_ref):
        return jnp.where(q_keep, q_ref[0].astype(F32), 0.0).astype(BF16)

    q_sb = block_diag_q(qs_ref)
    q_d = block_diag_q(qd_ref)

    def sb_chunk(k, v, mask, tri_blk, feature_major):
        s = _dot(q_sb, k) if feature_major else _dot_nt(q_sb, k)
        a, run = _sb_weights(s, run_ref[...], tri_blk, mask)
        a = a.astype(BF16)
        acc_s_ref[...] += _dot_nt(a, v) if feature_major else _dot(a, v)
        run_ref[...] = run

    n_dh = w // LANES
    t_i = lax.broadcasted_iota(jnp.int32, (page, n_dh * page), 0)
    j_i = lax.broadcasted_iota(jnp.int32, (page, n_dh * page), 1)
    spread = jnp.where(j_i // n_dh == t_i, 1.0, 0.0).astype(BF16)
    own_head = (lax.broadcasted_iota(jnp.int32, (rows, n_dh * page), 1) % n_dh
                == (lax.broadcasted_iota(jnp.int32, (rows, n_dh * page), 0) % n_chunk_heads) // 2)

    def diff_chunk(k, v_rows, mask, feature_major):
        s = _dot(q_d, k) if feature_major else _dot_nt(q_d, k)
        if mask is not None:
            s = jnp.where(mask, s, NEG)
        pr, alpha, m_new, l_new = _softmax_step(s, m_ref[...], l_ref[...])
        pr = pr.astype(BF16)
        n_pg = pr.shape[1] // page
        stacked = jnp.concatenate([pr[:, i * page:(i + 1) * page] for i in range(n_pg)], axis=0)
        wide = _dot(stacked, spread)
        wide = jnp.concatenate(
            [jnp.where(own_head, wide[i * rows:(i + 1) * rows], 0.0) for i in range(n_pg)], axis=1)
        acc_d_ref[...] = alpha * acc_d_ref[...] + _dot(wide.astype(BF16), v_rows)
        m_ref[...] = m_new
        l_ref[...] = l_new

    def init():
        acc_s_ref[...] = jnp.zeros_like(acc_s_ref)
        run_ref[...] = jnp.zeros_like(run_ref)
        acc_d_ref[...] = jnp.zeros_like(acc_d_ref)
        m_ref[...] = jnp.full_like(m_ref, NEG)
        l_ref[...] = jnp.zeros_like(l_ref)
        def padded(ref, n_rows):
            x = ref[0]
            pad = jnp.zeros((n_rows - x.shape[0], x.shape[1]), x.dtype)
            return jnp.concatenate([x, pad], axis=0).astype(BF16)
        key = lax.broadcasted_iota(jnp.int32, (rows, page), 1)
        qpos = lax.broadcasted_iota(jnp.int32, (rows, page), 0) // n_chunk_heads
        sb_chunk(padded(nks_ref, page), padded(nvs_ref, page), key < qpos, _suffix_sum_matrix(page), False)
        diff_chunk(padded(nkd_ref, page), padded(nvd_ref, n_dh * page), key <= qpos, False)

    def gathered(page_refs, axis):
        return jnp.concatenate([r[...].astype(BF16) for r in page_refs], axis=axis)

    def main():
        sb_chunk(gathered(caches[0], 1), gathered(caches[1], 1), None, tri, True)
        diff_chunk(gathered(caches[2], 1), gathered(caches[3], 0), None, True)

    def fin():
        lam = _diff_lambda(lam_ref, lambda_init)
        o_sb = jnp.where(q_keep, acc_s_ref[...], 0.0)
        r_d = lax.broadcasted_iota(jnp.int32, acc_d_ref.shape, 0) % n_chunk_heads
        coef = jnp.where(r_d % 2 == 0, 1.0, -lam) / l_ref[...]
        o_d = acc_d_ref[...] * coef

        def per_query(x):
            return jnp.concatenate(
                [jnp.sum(x[q * n_chunk_heads:(q + 1) * n_chunk_heads], axis=0, keepdims=True)
                 for q in range(ts)], axis=0)

        o_sb = per_query(o_sb)
        normed = [_rms(per_query(jnp.where(r_d // 2 == h, o_d, 0.0)), sub_ref[...]) * (1.0 - lambda_init)
                  for h in range(n_dh)]
        o_ref[0] = jnp.concatenate([o_sb] + normed, axis=1).astype(o_ref.dtype)

    return ((jnp.logical_and(active, c == 0), init), [(active, main)],
            (jnp.logical_and(active, c == n_chunks - 1), fin))


def _mixer_attn_kernel(pt_ref, pb_ref, pi_ref, pj_ref, *refs, pps, ts, n_chunks, page, lambda_init,
                       tq, n_pairs, n_sample_steps):
    del pt_ref, pb_ref
    s = pl.program_id(0)
    caches = [refs[c * pps:(c + 1) * pps] for c in range(4)]
    rest = refs[4 * pps:]
    sample_in, lam_ref, sub_ref, prompt_in = rest[:6], rest[6], rest[7], rest[8:14]
    o_s_ref, o_p_ref = rest[14:16]
    sample_scr, prompt_scr = rest[16:21], rest[21:25]
    sample = _sample_step(s % n_chunks, s < n_sample_steps, caches, *sample_in, lam_ref, sub_ref,
                          o_s_ref, *sample_scr, ts=ts, n_chunks=n_chunks, page=page,
                          lambda_init=lambda_init)
    prompt = _prompt_step(pi_ref[s], pj_ref[s], *prompt_in, lam_ref, sub_ref, o_p_ref, *prompt_scr,
                          tq=tq, lambda_init=lambda_init, n_pairs=n_pairs)
    for cond, fn in [sample[0], prompt[0]] + sample[1] + prompt[1] + [sample[2], prompt[2]]:
        pl.when(cond)(fn)


def _mixer_attention(page_table, caches, layer, qs_exp, qd_exp, new_kv, prompt_qkv, lam_vecs, subln,
                     lambda_init):
    bs, n_pages = page_table.shape
    w, page = caches[0].shape[2], caches[0].shape[3]
    rows = qs_exp.shape[1]
    ts = rows // (w // 64)
    pps = PAGES_PER_STEP
    assert n_pages % pps == 0 and (pps * page) % MXU_DIM == 0 and MXU_DIM % page == 0
    n_chunks = n_pages // pps
    n_sample_steps = bs * n_chunks
    pqs, pqd, pks, pvs, pkd, pvd = prompt_qkv
    bp, t, _ = pqs.shape
    tq = ATTN_TILE
    n_pairs = w // LANES
    assert t % tq == 0 and w % LANES == 0 and pks.shape == (bp, t // tq, w, tq)
    sched = [(b, i, j) for b in range(bp) for i in range(t // tq) for j in range(i, -1, -1)]
    n_steps = len(sched)
    assert n_steps >= n_sample_steps
    pb, pi, pj = (jnp.asarray([e[k] for e in sched], jnp.int32) for k in range(3))

    def sample_pos(s):
        s = jnp.minimum(s, n_sample_steps - 1)
        return s // n_chunks, s % n_chunks

    def page_spec(i, shape):
        def imap(s, pt, pb, pi, pj):
            b, c = sample_pos(s)
            return (layer, pt[b * n_pages + (n_chunks - 1 - c) * pps + i], 0, 0)
        return pl.BlockSpec((None, None) + shape, imap)

    in_specs, args = [], []
    for cache in caches:
        for i in range(pps):
            in_specs.append(page_spec(i, cache.shape[2:]))
            args.append(cache)
    per_b = lambda shape: pl.BlockSpec((1,) + shape, lambda s, pt, pb, pi, pj: (sample_pos(s)[0], 0, 0))
    in_specs += [per_b((rows, w)), per_b((rows, w))] + [per_b(a.shape[1:]) for a in new_kv]
    args += [qs_exp, qd_exp] + list(new_kv)
    in_specs += [pl.BlockSpec(lam_vecs.shape, lambda s, pt, pb, pi, pj: (0, 0)),
                 pl.BlockSpec(subln.shape, lambda s, pt, pb, pi, pj: (0, 0))]
    args += [lam_vecs, subln]
    qspec = pl.BlockSpec((1, tq, w), lambda s, pt, pb, pi, pj: (pb[s], pi[s], 0))
    fspec = pl.BlockSpec((1, 1, w, tq), lambda s, pt, pb, pi, pj: (pb[s], pj[s], 0, 0))
    vspec = pl.BlockSpec((1, tq, w), lambda s, pt, pb, pi, pj: (pb[s], pj[s], 0))
    in_specs += [qspec, qspec, fspec, fspec, fspec, vspec]
    args += [pqs, pqd, pks, pvs, pkd, pvd]
    grid_spec = pltpu.PrefetchScalarGridSpec(
        num_scalar_prefetch=4, grid=(n_steps,), in_specs=in_specs,
        out_specs=[pl.BlockSpec((1, ts, 2 * w), lambda s, pt, pb, pi, pj: (sample_pos(s)[0], 0, 0)),
                   pl.BlockSpec((1, tq, 2 * w), lambda s, pt, pb, pi, pj: (pb[s], pi[s], 0))],
        scratch_shapes=[pltpu.VMEM((rows, w), F32), pltpu.VMEM((rows, LANES), F32),
                        pltpu.VMEM((rows, LANES), F32), pltpu.VMEM((rows, LANES), F32),
                        pltpu.VMEM((rows, LANES), F32),
                        pltpu.VMEM((2 * n_pairs, 2 * tq, LANES), BF16),
                        pltpu.VMEM((2 * n_pairs, 2 * tq, LANES), F32),
                        pltpu.VMEM((2 * n_pairs, 2 * tq, LANES), F32),
                        pltpu.VMEM((n_pairs, 2 * tq, LANES), F32)])
    return pl.pallas_call(
        functools.partial(_mixer_attn_kernel, pps=pps, ts=ts, n_chunks=n_chunks, page=page,
                          lambda_init=lambda_init, tq=tq, n_pairs=n_pairs,
                          n_sample_steps=n_sample_steps),
        grid_spec=grid_spec,
        out_shape=[jax.ShapeDtypeStruct((bs, ts, 2 * w), BF16),
                   jax.ShapeDtypeStruct((bp, t, 2 * w), BF16)],
        compiler_params=_params(("arbitrary",)), name="mixer_attention",
    )(page_table.reshape(-1), pb, pi, pj, *args)


def _mem_attn_kernel(q_ref, mk_ref, mv_ref, o_ref, *, n_heads):
    hd = q_ref.shape[2] // n_heads
    for h in range(n_heads):
        cols = slice(h * hd, (h + 1) * hd)
        o = _softmax_attend(q_ref[0, :, cols], mk_ref[0, :, cols].astype(BF16),
                            mv_ref[0, :, cols].astype(BF16))
        o_ref[0, :, cols] = o.astype(o_ref.dtype)


def _softmax_attend(q, k, v):
    s = _dot_nt(q, k)
    p = jnp.exp(s - jnp.max(s, axis=-1, keepdims=True))
    return _dot(p.astype(BF16), v) / jnp.sum(p, axis=-1, keepdims=True)


def _mem_attention(q, mk, mv, n_heads):
    b, t, w = q.shape
    tm = min(TOKEN_TILE, t)
    assert t % tm == 0
    m = mk.shape[1]
    return pl.pallas_call(
        functools.partial(_mem_attn_kernel, n_heads=n_heads), grid=(b, t // tm),
        in_specs=[pl.BlockSpec((1, tm, w), lambda bi, i: (bi, i, 0)),
                  pl.BlockSpec((1, m, w), lambda bi, i: (bi, 0, 0)),
                  pl.BlockSpec((1, m, w), lambda bi, i: (bi, 0, 0))],
        out_specs=pl.BlockSpec((1, tm, w), lambda bi, i: (bi, i, 0)),
        out_shape=jax.ShapeDtypeStruct((b, t, w), BF16),
        compiler_params=_params(("parallel", "arbitrary")), name="mem_attention",
    )(q, mk, mv)


def _mixer_out_mem_kernel(o_ref, h_ref, wout_ref, g_ref, wq_ref, mk_ref, mv_ref, wo_ref, out_ref,
                          om_ref, *, n_heads, scale):
    h1 = h_ref[0] + _dot(o_ref[0], wout_ref[...])
    q = (_dot(_rms(h1, g_ref[...]).astype(BF16), wq_ref[...]) * scale).astype(BF16)
    hd = q.shape[1] // n_heads
    for h in range(n_heads):
        cols = slice(h * hd, (h + 1) * hd)
        om_ref[:, cols] = _softmax_attend(q[:, cols], mk_ref[0, :, cols].astype(BF16),
                                          mv_ref[0, :, cols].astype(BF16)).astype(BF16)
    out_ref[0] = h1 + _dot(om_ref[...], wo_ref[...])


def _mixer_out_mem(o, h, w_out, g_mem, w_q, mk, mv, w_o, n_heads, scale):
    b, t, d = h.shape
    tm = min(TOKEN_TILE, t)
    assert t % tm == 0
    m, w = mk.shape[1:]
    tile = lambda last: pl.BlockSpec((1, tm, last), lambda bi, i: (bi, i, 0))
    return pl.pallas_call(
        functools.partial(_mixer_out_mem_kernel, n_heads=n_heads, scale=scale), grid=(b, t // tm),
        in_specs=[tile(o.shape[2]), tile(d), _resident(w_out.shape), _resident((1, d)),
                  _resident(w_q.shape), pl.BlockSpec((1, m, w), lambda bi, i: (bi, 0, 0)),
                  pl.BlockSpec((1, m, w), lambda bi, i: (bi, 0, 0)), _resident(w_o.shape)],
        out_specs=tile(d), out_shape=jax.ShapeDtypeStruct((b, t, d), F32),
        scratch_shapes=[pltpu.VMEM((tm, w), BF16)],
        compiler_params=_params(("parallel", "arbitrary")), name="mixer_out_mem",
    )(o, h, w_out, g_mem.reshape(1, d), w_q, mk, mv, w_o)


def _ffn_kernel(h_ref, cin_ref, g_ref, wup_ref, wconv_ref, bconv_ref, wdown_ref, gfin_ref,
                y_ref, cout_ref, ubuf_g_ref, ubuf_v_ref, act_ref, *, shift, dff, final_norm):
    t = pl.program_id(1)
    tm = h_ref.shape[1]
    cr = cout_ref.shape[1]

    @pl.when(t == 0)
    def _():
        cout_ref[0] = cin_ref[0]

    h = h_ref[0]
    hn = _rms(h, g_ref[...]).astype(BF16)

    def conv(ubuf_ref, c0):
        cols = slice(c0, c0 + FF_CHUNK)
        u = _dot(hn, wup_ref[:, cols])
        ubuf_ref[0:cr, :] = cout_ref[0, :, cols]
        ubuf_ref[cr:cr + tm, :] = u
        out = bconv_ref[:, cols] + u * wconv_ref[CONV_W - 1:CONV_W, cols]
        for i in range(CONV_W - 1):
            back = (CONV_W - 1 - i) * shift
            out = out + ubuf_ref[cr - back:cr - back + tm, :] * wconv_ref[i:i + 1, cols]
        cout_ref[0, :, cols] = ubuf_ref[tm:tm + cr, :]
        return out

    for j in range(dff // FF_CHUNK):
        gate = conv(ubuf_g_ref, j * FF_CHUNK)
        val = conv(ubuf_v_ref, dff + j * FF_CHUNK)
        act = gate * (1.0 / (1.0 + jnp.exp(-gate))) * val
        act_ref[:, j * FF_CHUNK:(j + 1) * FF_CHUNK] = act.astype(BF16)

    out = h + _dot(act_ref[...], wdown_ref[...])
    y_ref[0] = _rms(out, gfin_ref[...]) if final_norm else out


def _conv_ffn(h, conv_in, g, w_up, w_conv, b_conv, w_down, g_final, final_norm, shift):
    ngrp, t, d = h.shape
    dff = w_down.shape[0]
    cr = conv_in.shape[1]
    tm = min(TOKEN_TILE, t)
    assert t % tm == 0 and dff % FF_CHUNK == 0 and cr >= (CONV_W - 1) * shift and tm >= cr
    return pl.pallas_call(
        functools.partial(_ffn_kernel, shift=shift, dff=dff, final_norm=final_norm),
        grid=(ngrp, t // tm),
        in_specs=[pl.BlockSpec((1, tm, d), lambda gi, i: (gi, i, 0)),
                  pl.BlockSpec((1, cr, 2 * dff), lambda gi, i: (gi, 0, 0),
                               pipeline_mode=pl.Buffered(1)),
                  _resident((1, d)), _resident(w_up.shape), _resident(w_conv.shape),
                  _resident((1, 2 * dff)), _resident(w_down.shape), _resident((1, d))],
        out_specs=[pl.BlockSpec((1, tm, d), lambda gi, i: (gi, i, 0)),
                   pl.BlockSpec((1, cr, 2 * dff), lambda gi, i: (gi, 0, 0))],
        out_shape=[jax.ShapeDtypeStruct((ngrp, t, d), F32),
                   jax.ShapeDtypeStruct((ngrp, cr, 2 * dff), F32)],
        scratch_shapes=[pltpu.VMEM((cr + tm, FF_CHUNK), F32), pltpu.VMEM((cr + tm, FF_CHUNK), F32),
                        pltpu.VMEM((tm, dff), BF16)],
        compiler_params=_params(("parallel", "arbitrary")), name="conv_ffn",
    )(h, conv_in, g.reshape(1, d), w_up, w_conv, b_conv.reshape(1, 2 * dff), w_down,
      g_final.reshape(1, d))


def kernel(x_prompt, x_sample, cache_sb_k, cache_sb_v, cache_diff_k, cache_diff_v, cache_mem_k, cache_mem_v, state_conv, page_table, mem_prompt, norm_mix, w_in, lambda_q1, lambda_k1, lambda_q2, lambda_k2, diff_subln, w_out, norm_mem_q, norm_mem_kv, w_mem_q, w_mem_k, w_mem_v, w_mem_o, norm_ffn, w_up, w_conv, b_conv, w_down, norm_final):
    bp, tp, d = x_prompt.shape
    bs, ts, _ = x_sample.shape
    depth, n_pool, page, sb_heads, sb_hd = cache_sb_k.shape
    diff_heads, diff_hd = cache_diff_k.shape[3], cache_diff_k.shape[5]
    mem_tokens, mem_heads, mem_hd = cache_mem_k.shape[2:]
    n_pages = page_table.shape[1]
    past_len = n_pages * page
    sb_w = sb_heads * sb_hd
    dqk_w = diff_heads * 2 * diff_hd
    dv_w = diff_heads * 2 * diff_hd
    mem_w = mem_heads * mem_hd
    dff = w_down.shape[1]
    assert sb_hd == 64 and diff_hd == 64 and sb_w == dqk_w == dv_w and ts <= SUBLANES
    w = sb_w
    n_chunk_heads = w // 64

    sb_scale = math.log2(math.e) / math.sqrt(sb_hd)
    diff_scale = math.log2(math.e) / math.sqrt(diff_hd)
    mem_scale = 1.0 / math.sqrt(mem_hd)

    in_segs = [(0, w, sb_scale, False, False, True), (w, w, 1.0, False, True, True),
               (2 * w, w, 1.0, False, True, True), (3 * w, w, diff_scale, True, False, True),
               (4 * w, w, 1.0, True, True, True), (5 * w, w, 1.0, False, True, True)]
    rope_p = _rope_table(jnp.arange(tp, dtype=jnp.int32))
    rope_pt = _rope_table_t(jnp.arange(tp, dtype=jnp.int32))
    rope_s = jnp.tile(_rope_table(past_len + jnp.arange(ts, dtype=jnp.int32)), (bs, 1))

    hp = x_prompt.reshape(bp * tp, d)
    hs = x_sample.reshape(bs * ts, d)
    outs = {k: [] for k in ("p_sbk", "p_sbv", "p_dk", "p_dv", "p_mk", "p_mv", "p_conv",
                            "s_sbk", "s_sbv", "s_dk", "s_dv", "s_conv")}
    conv_rows_p = SUBLANES
    for l in range(depth):
        lambda_init = 0.8 - 0.6 * math.exp(-0.3 * l)
        lam_vecs = jnp.stack([lambda_q1[l], lambda_k1[l], lambda_q2[l], lambda_k2[l]]).astype(F32)
        subln = diff_subln[l].reshape(1, -1)
        w_in_b = w_in[l].astype(BF16)
        w_out_b = w_out[l].astype(BF16)
        w_q_b = w_mem_q[l].astype(BF16)
        w_kv_b = jnp.concatenate([w_mem_k[l], w_mem_v[l]], axis=1).astype(BF16)
        w_o_b = w_mem_o[l].astype(BF16)
        w_up_b = w_up[l].astype(BF16)
        w_down_b = w_down[l].astype(BF16)
        q_seg = [(0, mem_w, mem_scale, False, False, True)]

        def mem_block(h2d, ngrp, mk, mv):
            (q,) = _norm_matmul(h2d, norm_mem_q[l], w_q_b, q_seg)
            t = h2d.shape[0] // ngrp
            o = _mem_attention(q.reshape(ngrp, t, mem_w), mk, mv, mem_heads)
            return _matmul_residual(o.reshape(ngrp * t, mem_w), w_o_b, h2d)

        qs, qd, vdf, vdb, kst, ksb, vst, vsb, kdt, kdb = _prompt_in_proj(
            hp.reshape(bp, tp, d), norm_mix[l], w_in_b, w_in[l].T.astype(BF16), rope_p, rope_pt, w,
            sb_scale, diff_scale)
        s_qs, s_ksf, _, s_vsf, _, s_qd, s_kdf, _, s_vdf, _ = _norm_matmul(
            hs, norm_mix[l], w_in_b, in_segs, rope_s)
        expand = lambda q: jnp.repeat(q.reshape(bs, ts, w), n_chunk_heads, axis=1)
        pad_new = lambda a: jnp.pad(a.reshape(bs, ts, w), ((0, 0), (0, SUBLANES - ts), (0, 0)))
        caches = [jnp.moveaxis(cache_sb_k, 2, -1).reshape(depth, n_pool, w, page),
                  jnp.moveaxis(cache_sb_v, 2, -1).reshape(depth, n_pool, w, page),
                  jnp.moveaxis(cache_diff_k, 2, -1).reshape(depth, n_pool, w, page),
                  cache_diff_v.reshape(depth, n_pool, page * diff_heads, 2 * diff_hd)]
        new_kv = [pad_new(a) for a in (s_ksf, s_vsf, s_kdf, s_vdf)]
        new_kv[3] = new_kv[3].reshape(bs, SUBLANES * diff_heads, 2 * diff_hd)
        o_s, o = _mixer_attention(page_table, caches, l, expand(s_qs), expand(s_qd), new_kv,
                                  (qs, qd, ksb, vsb, kdb, vdb), lam_vecs, subln, lambda_init)

        mkf, mvf = _norm_matmul(mem_prompt.reshape(bp * mem_tokens, d), norm_mem_kv[l], w_kv_b,
                                [(0, mem_w, 1.0, False, True, False),
                                 (mem_w, mem_w, 1.0, False, True, False)])
        mk3 = mkf.reshape(bp, mem_tokens, mem_w)
        mv3 = mvf.reshape(bp, mem_tokens, mem_w)
        hp3 = _mixer_out_mem(o, hp.reshape(bp, tp, d), w_out_b, norm_mem_q[l], w_q_b, mk3, mv3, w_o_b,
                             mem_heads, mem_scale)
        conv0 = jnp.zeros((bp, conv_rows_p, 2 * dff), F32)
        last = l == depth - 1
        yp, conv_p = _conv_ffn(hp3, conv0, norm_ffn[l], w_up_b, w_conv[l],
                               b_conv[l], w_down_b, norm_final, last, 1)
        outs["p_sbk"].append(jnp.moveaxis(kst.reshape(bp, sb_heads, sb_hd, tp), -1, 1))
        outs["p_sbv"].append(jnp.moveaxis(vst.reshape(bp, sb_heads, sb_hd, tp), -1, 1))
        outs["p_dk"].append(jnp.moveaxis(kdt.reshape(bp, diff_heads, 2, diff_hd, tp), -1, 1))
        outs["p_dv"].append(vdf.reshape(bp, tp, diff_heads, 2 * diff_hd))
        outs["p_mk"].append(mk3.reshape(bp, mem_tokens, mem_heads, mem_hd))
        outs["p_mv"].append(mv3.reshape(bp, mem_tokens, mem_heads, mem_hd))
        outs["p_conv"].append(conv_p[:, conv_rows_p - (CONV_W - 1):])

        hs = _matmul_residual(o_s.reshape(bs * ts, 2 * w), w_out_b, hs)
        hs = mem_block(hs, bs, cache_mem_k[l].reshape(bs, mem_tokens, mem_w),
                       cache_mem_v[l].reshape(bs, mem_tokens, mem_w))
        hs_tm = hs.reshape(bs, ts, d).transpose(1, 0, 2).reshape(1, ts * bs, d)
        conv_in = state_conv[l].transpose(1, 0, 2).reshape(1, (CONV_W - 1) * bs, 2 * dff)
        ys, conv_s = _conv_ffn(hs_tm, conv_in, norm_ffn[l], w_up_b, w_conv[l], b_conv[l], w_down_b,
                               norm_final, last, bs)
        hs = ys.reshape(ts, bs, d).transpose(1, 0, 2).reshape(bs * ts, d)
        outs["s_sbk"].append(s_ksf.reshape(bs, ts, sb_heads, sb_hd))
        outs["s_sbv"].append(s_vsf.reshape(bs, ts, sb_heads, sb_hd))
        outs["s_dk"].append(s_kdf.reshape(bs, ts, diff_heads, 2, diff_hd))
        outs["s_dv"].append(s_vdf.reshape(bs, ts, diff_heads, 2 * diff_hd))
        outs["s_conv"].append(conv_s.reshape(CONV_W - 1, bs, 2 * dff).transpose(1, 0, 2))
        hp = yp.reshape(bp * tp, d)

    st = lambda xs: jnp.stack(xs, axis=0)
    return (hp.reshape(bp, tp, d), hs.reshape(bs, ts, d), st(outs["p_sbk"]), st(outs["p_sbv"]),
            st(outs["p_dk"]), st(outs["p_dv"]), st(outs["p_mk"]), st(outs["p_mv"]), st(outs["p_conv"]),
            st(outs["s_sbk"]), st(outs["s_sbv"]), st(outs["s_dk"]), st(outs["s_dv"]), st(outs["s_conv"]))
```
